```python
import math
import jax, jax.numpy as jnp
from jax import lax
import numpy as np


D_MODEL = 1024
BATCH = 8
SEQ = 4096
DEPTH = 4

D_ATTN = D_MODEL // 2
HEAD_DIM = 64
N_HEADS = D_ATTN // HEAD_DIM
ROPE_DIM = HEAD_DIM // 4
ROPE_THETA = 500000.0
DILATED_PATTERNS = ((128, 1), (512, 4), (2048, 16))
D_SSM = D_MODEL - D_ATTN
SSM_GROUP = 16
N_SSM_GROUPS = D_SSM // SSM_GROUP
SSM_STATE = 64
DT_MIN = 0.001
DT_MAX = 0.1
D_MIX = D_ATTN + D_SSM
D_IN_PROJ = 3 * D_ATTN + D_SSM
D_FF = 128 * (-(-(8 * D_MODEL // 3) // 128))
PLE_DIM = 256
NORM_EPS = 1e-6

kernel_name = 'hybrid_s5_dilated_macaron_block'


def rms_norm(x, g):
    xf = x.astype(jnp.float32)
    y = xf * lax.rsqrt(jnp.mean(xf * xf, axis=-1, keepdims=True) + NORM_EPS)
    return (y * g.astype(jnp.float32)).astype(x.dtype)


def swiglu(x, w_gate, w_up, w_down):
    return (jax.nn.silu(x @ w_gate) * (x @ w_up)) @ w_down


def partial_rotary(t, positions):
    half = ROPE_DIM // 2
    inv_freq = ROPE_THETA ** (-jnp.arange(half, dtype=jnp.float32) * (2.0 / ROPE_DIM))
    ang = positions.astype(jnp.float32)[:, :, None, None] * inv_freq
    cos, sin = jnp.cos(ang), jnp.sin(ang)
    tf = t.astype(jnp.float32)
    t1, t2, rest = tf[..., :half], tf[..., half:ROPE_DIM], tf[..., ROPE_DIM:]
    out = jnp.concatenate([t1 * cos - t2 * sin, t2 * cos + t1 * sin, rest], axis=-1)
    return out.astype(t.dtype)


def dilated_band_attention(q, k, v, window, dilation):
    b_, s_, h_, dh = q.shape
    band = window // dilation
    n_str = s_ // dilation
    nb = -(-n_str // band)
    lp = nb * band
    nrows = b_ * dilation

    def to_blocks(t):
        t = t.reshape(b_, n_str, dilation, h_, dh).transpose(0, 2, 1, 3, 4).reshape(nrows, n_str, h_, dh)
        t = jnp.pad(t, ((0, 0), (0, lp - n_str), (0, 0), (0, 0)))
        return t.reshape(nrows, nb, band, h_, dh)

    def with_prev(t):
        prev = jnp.pad(t[:, :-1], ((0, 0), (1, 0), (0, 0), (0, 0), (0, 0)))
        return jnp.concatenate([prev, t], axis=2)

    qb = to_blocks(q)
    kc = with_prev(to_blocks(k))
    vc = with_prev(to_blocks(v))
    scores = jnp.einsum('nbqhd,nbkhd->nbhqk', qb, kc).astype(jnp.float32) * (HEAD_DIM ** -0.5)
    qi = jnp.arange(band)[:, None]
    kj = jnp.arange(2 * band)[None, :]
    dist = qi + band - kj
    band_ok = (dist >= 0) & (dist <= band)
    blk = jnp.arange(nb)[:, None, None]
    mask = band_ok[None] & ((blk > 0) | (kj >= band)[None])
    scores = jnp.where(mask[None, :, None], scores, -jnp.inf)
    m = jnp.max(scores, axis=-1, keepdims=True)
    e = jnp.exp(scores - m)
    den = jnp.sum(e, axis=-1, keepdims=True)
    probs = (e / den).astype(v.dtype)
    out = jnp.einsum('nbhqk,nbkhd->nbqhd', probs, vc).astype(jnp.float32)
    lse = (m + jnp.log(den))[..., 0]
    out = out.reshape(nrows, lp, h_, dh)[:, :n_str]
    out = out.reshape(b_, dilation, n_str, h_, dh).transpose(0, 2, 1, 3, 4).reshape(b_, s_, h_, dh)
    lse = lse.transpose(0, 1, 3, 2).reshape(nrows, lp, h_)[:, :n_str]
    lse = lse.reshape(b_, dilation, n_str, h_).transpose(0, 2, 1, 3).reshape(b_, s_, h_)
    return out, lse


def dilated_mixture_attention(q, k, v):
    outs, lses = [], []
    for window, dilation in DILATED_PATTERNS:
        o, l = dilated_band_attention(q, k, v, window, dilation)
        outs.append(o)
        lses.append(l)
    wts = jax.nn.softmax(jnp.stack(lses, axis=0), axis=0)
    out = jnp.sum(wts[..., None] * jnp.stack(outs, axis=0), axis=0)
    b_, s_ = q.shape[0], q.shape[1]
    return out.reshape(b_, s_, D_ATTN).astype(q.dtype)


def s5_mixer(u, lam_re, lam_im, log_dt, b_re, b_im, c_re, c_im, d_skip, w_glu, b_glu):
    b_, s_ = u.shape[0], u.shape[1]
    uf = u.astype(jnp.float32).reshape(b_, s_, N_SSM_GROUPS, SSM_GROUP)
    lam = lax.complex(lam_re.astype(jnp.float32), lam_im.astype(jnp.float32))
    dt = jnp.exp(log_dt.astype(jnp.float32))[:, None]
    lam_bar = jnp.exp(lam * dt)
    b_mat = lax.complex(b_re.astype(jnp.float32), b_im.astype(jnp.float32))
    b_bar = ((lam_bar - 1.0) / lam)[..., None] * b_mat
    bu = jnp.einsum('bsgh,gph->bsgp', uf.astype(jnp.complex64), b_bar)
    a = jnp.broadcast_to(lam_bar, bu.shape)

    def combine(left, right):
        a_l, x_l = left
        a_r, x_r = right
        return a_r * a_l, a_r * x_l + x_r

    _, states = lax.associative_scan(combine, (a, bu), axis=1)
    c_mat = lax.complex(c_re.astype(jnp.float32), c_im.astype(jnp.float32))
    y = jnp.real(jnp.einsum('bsgp,ghp->bsgh', states, c_mat)) + d_skip.astype(jnp.float32) * uf
    y = jax.nn.gelu(y).reshape(b_, s_, D_SSM)
    y = y * jax.nn.sigmoid(y @ w_glu.astype(jnp.float32) + b_glu.astype(jnp.float32))
    return y.astype(u.dtype)


def setup_inputs(seed: int = 0) -> dict:
    key = jax.random.key(seed)
    ks = jax.random.split(key, 40)
    f32 = jnp.float32

    def nrm(k, shape, fan_in):
        return jax.random.normal(k, shape, f32) * (fan_in ** -0.5)

    def gain(k, shape):
        return 1.0 + 0.05 * jax.random.normal(k, shape, f32)

    lam_im_base = jnp.pi * jnp.arange(SSM_STATE, dtype=f32)
    return {
        'x': jax.random.normal(ks[0], (BATCH, SEQ, D_MODEL), f32),
        'p': jax.random.normal(ks[1], (DEPTH, BATCH, SEQ, PLE_DIM), f32),
        'positions': (jax.random.randint(ks[2], (BATCH, 1), 0, 1024, dtype=jnp.int32)
                      + jnp.arange(SEQ, dtype=jnp.int32)[None, :]),
        'ffn1_pre_g': gain(ks[3], (DEPTH, D_MODEL)),
        'ffn1_w_gate': nrm(ks[4], (DEPTH, D_MODEL, D_FF), D_MODEL),
        'ffn1_w_up': nrm(ks[5], (DEPTH, D_MODEL, D_FF), D_MODEL),
        'ffn1_w_down': nrm(ks[6], (DEPTH, D_FF, D_MODEL), D_FF),
        'ffn1_post_g': gain(ks[7], (DEPTH, D_MODEL)),
        'mix_pre_g': gain(ks[8], (DEPTH, D_MODEL)),
        'w_in': nrm(ks[9], (DEPTH, D_MODEL, D_IN_PROJ), D_MODEL),
        'attn_norm_g': gain(ks[10], (DEPTH, D_ATTN)),
        'ssm_lam_re': -0.5 + 0.01 * jax.random.normal(ks[11], (DEPTH, N_SSM_GROUPS, SSM_STATE), f32),
        'ssm_lam_im': lam_im_base + 0.01 * jax.random.normal(ks[12], (DEPTH, N_SSM_GROUPS, SSM_STATE), f32),
        'ssm_log_dt': jax.random.uniform(ks[13], (DEPTH, N_SSM_GROUPS), f32,
                                         minval=math.log(DT_MIN), maxval=math.log(DT_MAX)),
        'ssm_b_re': nrm(ks[14], (DEPTH, N_SSM_GROUPS, SSM_STATE, SSM_GROUP), 2 * SSM_GROUP),
        'ssm_b_im': nrm(ks[15], (DEPTH, N_SSM_GROUPS, SSM_STATE, SSM_GROUP), 2 * SSM_GROUP),
        'ssm_c_re': nrm(ks[16], (DEPTH, N_SSM_GROUPS, SSM_GROUP, SSM_STATE), 2 * SSM_STATE),
        'ssm_c_im': nrm(ks[17], (DEPTH, N_SSM_GROUPS, SSM_GROUP, SSM_STATE), 2 * SSM_STATE),
        'ssm_d': jax.random.normal(ks[18], (DEPTH, N_SSM_GROUPS, SSM_GROUP), f32),
        'ssm_w_glu': nrm(ks[19], (DEPTH, D_SSM, D_SSM), D_SSM),
        'ssm_b_glu': 0.01 * jax.random.normal(ks[20], (DEPTH, D_SSM), f32),
        'ssm_norm_g': gain(ks[21], (DEPTH, D_SSM)),
        'w_out': nrm(ks[22], (DEPTH, D_MIX, D_MODEL), D_MIX),
        'mix_post_g': gain(ks[23], (DEPTH, D_MODEL)),
        'ffn2_pre_g': gain(ks[24], (DEPTH, D_MODEL)),
        'ffn2_w_gate': nrm(ks[25], (DEPTH, D_MODEL, D_FF), D_MODEL),
        'ffn2_w_up': nrm(ks[26], (DEPTH, D_MODEL, D_FF), D_MODEL),
        'ffn2_w_down': nrm(ks[27], (DEPTH, D_FF, D_MODEL), D_FF),
        'ffn2_post_g': gain(ks[28], (DEPTH, D_MODEL)),
        'ple_w_up': nrm(ks[29], (DEPTH, PLE_DIM, D_MODEL), PLE_DIM),
        'ple_w_gate': nrm(ks[30], (DEPTH, D_MODEL, D_MODEL), D_MODEL),
        'ple_post_g': gain(ks[31], (DEPTH, D_MODEL)),
    }


def reference(x, p, positions,
              ffn1_pre_g, ffn1_w_gate, ffn1_w_up, ffn1_w_down, ffn1_post_g,
              mix_pre_g, w_in, attn_norm_g,
              ssm_lam_re, ssm_lam_im, ssm_log_dt, ssm_b_re, ssm_b_im, ssm_c_re, ssm_c_im,
              ssm_d, ssm_w_glu, ssm_b_glu, ssm_norm_g, w_out, mix_post_g,
              ffn2_pre_g, ffn2_w_gate, ffn2_w_up, ffn2_w_down, ffn2_post_g,
              ple_w_up, ple_w_gate, ple_post_g):
    b_, s_ = x.shape[0], x.shape[1]
    h = x
    for i in range(DEPTH):
        f = swiglu(rms_norm(h, ffn1_pre_g[i]), ffn1_w_gate[i], ffn1_w_up[i], ffn1_w_down[i])
        h = h + 0.5 * rms_norm(f, ffn1_post_g[i])

        a_in = rms_norm(h, mix_pre_g[i])
        proj = a_in @ w_in[i]
        q, k, v, u = jnp.split(proj, [D_ATTN, 2 * D_ATTN, 3 * D_ATTN], axis=-1)
        q = partial_rotary(q.reshape(b_, s_, N_HEADS, HEAD_DIM), positions)
        k = partial_rotary(k.reshape(b_, s_, N_HEADS, HEAD_DIM), positions)
        v = v.reshape(b_, s_, N_HEADS, HEAD_DIM)
        attn = dilated_mixture_attention(q, k, v)
        ssm = s5_mixer(u, ssm_lam_re[i], ssm_lam_im[i], ssm_log_dt[i], ssm_b_re[i], ssm_b_im[i],
                       ssm_c_re[i], ssm_c_im[i], ssm_d[i], ssm_w_glu[i], ssm_b_glu[i])
        mixed = jnp.concatenate([rms_norm(attn, attn_norm_g[i]), rms_norm(ssm, ssm_norm_g[i])], axis=-1)
        h = h + rms_norm(mixed @ w_out[i], mix_post_g[i])

        f = swiglu(rms_norm(h, ffn2_pre_g[i]), ffn2_w_gate[i], ffn2_w_up[i], ffn2_w_down[i])
        h = h + 0.5 * rms_norm(f, ffn2_post_g[i])

        ple = (p[i] @ ple_w_up[i]) * jax.nn.sigmoid(h @ ple_w_gate[i])
        h = h + rms_norm(ple, ple_post_g[i])
    return h
```

```python
import functools
import math

import numpy as np
import jax
import jax.numpy as jnp
from jax import lax
from jax.experimental import pallas as pl
from jax.experimental.pallas import tpu as pltpu

F32 = jnp.float32
BF16 = jnp.bfloat16

LANES = 128
SUBLANES = 8
MXU_DIM = 256

HEAD_DIM = 64
ROPE_DIM = HEAD_DIM // 4
ROPE_THETA = 500000.0
DILATED_PATTERNS = ((128, 1), (512, 4), (2048, 16))
BAND = 128
SSM_GROUP = 16
SSM_STATE = 64
NORM_EPS = 1e-6
MASK_VALUE = -1e30

assert all(w // d == BAND for w, d in DILATED_PATTERNS)
DILATIONS = tuple(d for _, d in DILATED_PATTERNS)

VMEM_LIMIT_BYTES = 56 * 1024 * 1024


def _rms_norm(x, g):
    ms = jnp.mean(x * x, axis=-1, keepdims=True)
    return x * lax.rsqrt(ms + NORM_EPS) * g


def _const_spec(shape):
    zeros = (0,) * len(shape)
    return pl.BlockSpec(shape, lambda *_: zeros, pipeline_mode=pl.Buffered(1))


def _params(*sem):
    return pltpu.CompilerParams(dimension_semantics=sem, vmem_limit_bytes=VMEM_LIMIT_BYTES)


def _ffn_body(x_ref, pre_ref, wg_ref, wu_ref, wd_ref, post_ref, o_ref, *, n_chunks):
    x = x_ref[...]
    xn = _rms_norm(x, pre_ref[...]).astype(BF16)
    ck = wg_ref.shape[1] // n_chunks
    acc = None
    for c in range(n_chunks):
        sl = slice(c * ck, (c + 1) * ck)
        g = jnp.dot(xn, wg_ref[:, sl], preferred_element_type=F32)
        u = jnp.dot(xn, wu_ref[:, sl], preferred_element_type=F32)
        mid = (g * jax.nn.sigmoid(g) * u).astype(BF16)
        part = jnp.dot(mid, wd_ref[sl, :], preferred_element_type=F32)
        acc = part if acc is None else acc + part
    o_ref[...] = x + 0.5 * _rms_norm(acc, post_ref[...])


def _ffn(h, pre_g, wg, wu, wd, post_g, *, tm):
    t, d = h.shape
    ff = wg.shape[1]
    n_chunks = 2 if ff % (2 * LANES) == 0 else 1
    row = pl.BlockSpec((tm, d), lambda i: (i, 0))
    return pl.pallas_call(
        functools.partial(_ffn_body, n_chunks=n_chunks),
        grid=(t // tm,),
        in_specs=[row, _const_spec((1, d)), _const_spec((d, ff)), _const_spec((d, ff)),
                  _const_spec((ff, d)), _const_spec((1, d))],
        out_specs=row,
        out_shape=jax.ShapeDtypeStruct((t, d), F32),
        compiler_params=_params("parallel"),
        name="ffn",
    )(h, pre_g, wg, wu, wd, post_g)


def _rope_tables():
    half = ROPE_DIM // 2
    inv_freq = ROPE_THETA ** (-jnp.arange(half, dtype=F32) * (2.0 / ROPE_DIM))
    lane = np.arange(LANES)
    in_head = lane % HEAD_DIM
    freq = jnp.where(in_head < ROPE_DIM, inv_freq[lane % half], 0.0).astype(F32)
    first = (in_head < half).astype(np.float32)
    second = ((in_head >= half) & (in_head < ROPE_DIM)).astype(np.float32)
    return freq.reshape(1, LANES), jnp.asarray(first).reshape(1, LANES), jnp.asarray(second).reshape(1, LANES)


def _inproj_body(x_ref, pos_ref, g_ref, w_ref, freq_ref, first_ref, second_ref,
                 q_ref, k_ref, v_ref, u_ref, *, d_attn):
    xn = _rms_norm(x_ref[...], g_ref[...]).astype(BF16)
    proj = jnp.dot(xn, w_ref[...], preferred_element_type=F32)
    ang = pos_ref[...] * freq_ref[...]
    cos = jnp.cos(ang)
    sin = jnp.sin(ang)
    half = ROPE_DIM // 2
    sin_first = -sin * first_ref[...]
    sin_second = sin * second_ref[...]

    def rotary(t):
        return (t * cos + pltpu.roll(t, LANES - half, 1) * sin_first
                + pltpu.roll(t, half, 1) * sin_second)

    scale = HEAD_DIM ** -0.5
    for j in range(d_attn // LANES):
        sl = slice(j * LANES, (j + 1) * LANES)
        q_ref[:, sl] = rotary(proj[:, j * LANES:(j + 1) * LANES]) * scale
        k_ref[:, sl] = rotary(proj[:, d_attn + j * LANES:d_attn + (j + 1) * LANES])
    v_ref[...] = proj[:, 2 * d_attn:3 * d_attn]
    u_ref[...] = proj[:, 3 * d_attn:]


def _inproj(h, pos, g, w_in, *, d_attn, tm):
    t, d = h.shape
    n = w_in.shape[1]
    d_ssm = n - 3 * d_attn
    freq, first, second = _rope_tables()
    row = lambda w: pl.BlockSpec((tm, w), lambda i: (i, 0))
    return pl.pallas_call(
        functools.partial(_inproj_body, d_attn=d_attn),
        grid=(t // tm,),
        in_specs=[row(d), row(1), _const_spec((1, d)), _const_spec((d, n)),
                  _const_spec((1, LANES)), _const_spec((1, LANES)), _const_spec((1, LANES))],
        out_specs=[row(d_attn), row(d_attn), row(d_attn), row(d_ssm)],
        out_shape=[jax.ShapeDtypeStruct((t, d_attn), F32)] * 3 + [jax.ShapeDtypeStruct((t, d_ssm), F32)],
        compiler_params=_params("parallel"),
        name="inproj",
    )(h, pos, g, w_in, freq, first, second)


def _attn_body(q_ref, k_ref, v_ref, o_ref, m_s, l_s, acc_s, *, seq):
    lane = lax.broadcasted_iota(jnp.int32, (BAND, LANES), 1)
    head0 = lane < HEAD_DIM
    qi = lax.broadcasted_iota(jnp.int32, (BAND, BAND), 0)
    kj = lax.broadcasted_iota(jnp.int32, (BAND, BAND), 1)
    cur_ok = kj <= qi
    prev_ok = kj >= qi

    m_s[...] = jnp.full(m_s.shape, MASK_VALUE, F32)
    l_s[...] = jnp.zeros(l_s.shape, F32)
    acc_s[...] = jnp.zeros(acc_s.shape, F32)

    for d in DILATIONS:
        nb = seq // (BAND * d)

        def rows(start):
            return pl.ds(start, BAND) if d == 1 else pl.ds(start, BAND, stride=d)

        def block(n, carry):
            r = n // nb
            i = n - r * nb
            start = r + i * (BAND * d)
            prev = jnp.maximum(start - BAND * d, r)
            q = q_ref[rows(start), :]
            zero = jnp.zeros_like(q)
            q2 = jnp.concatenate([jnp.where(head0, q, zero), jnp.where(head0, zero, q)], axis=0)
            kk = jnp.concatenate([k_ref[rows(prev), :], k_ref[rows(start), :]], axis=0)
            vv = jnp.concatenate([v_ref[rows(prev), :], v_ref[rows(start), :]], axis=0)
            s = lax.dot_general(q2.astype(BF16), kk.astype(BF16), (((1,), (1,)), ((), ())),
                                preferred_element_type=F32)
            ok = jnp.concatenate([prev_ok & (i > 0), cur_ok], axis=1)
            ok = jnp.concatenate([ok, ok], axis=0)
            s = jnp.where(ok, s, MASK_VALUE)
            m = jnp.max(s, axis=-1, keepdims=True)
            e = jnp.exp(s - m)
            l = jnp.sum(e, axis=-1, keepdims=True)
            o2 = jnp.dot(e.astype(BF16), vv.astype(BF16), preferred_element_type=F32)
            o_b = jnp.where(head0, o2[:BAND], o2[BAND:])
            m_b = jnp.where(head0, m[:BAND], m[BAND:])
            l_b = jnp.where(head0, l[:BAND], l[BAND:])
            m_old = m_s[rows(start), :]
            m_new = jnp.maximum(m_old, m_b)
            a_old = jnp.exp(m_old - m_new)
            a_b = jnp.exp(m_b - m_new)
            m_s[rows(start), :] = m_new
            l_s[rows(start), :] = a_old * l_s[rows(start), :] + a_b * l_b
            acc_s[rows(start), :] = a_old * acc_s[rows(start), :] + a_b * o_b
            return carry

        lax.fori_loop(0, d * nb, block, 0)

    o_ref[...] = acc_s[...] / l_s[...]


def _attention(q, k, v):
    b, s, d_attn = q.shape
    assert s % (BAND * max(DILATIONS)) == 0 and d_attn % LANES == 0
    spec = pl.BlockSpec((None, s, LANES), lambda bi, hi: (bi, 0, hi))
    return pl.pallas_call(
        functools.partial(_attn_body, seq=s),
        grid=(b, d_attn // LANES),
        in_specs=[spec, spec, spec],
        out_specs=spec,
        out_shape=jax.ShapeDtypeStruct((b, s, d_attn), F32),
        scratch_shapes=[pltpu.VMEM((s, LANES), F32)] * 3,
        compiler_params=_params("parallel", "parallel"),
        name="attention",
    )(q, k, v)


def _ssm_matrices(lam_re, lam_im, log_dt, b_re, b_im, c_re, c_im):
    g, p = lam_re.shape
    dt = jnp.exp(log_dt)[:, None]
    mag = jnp.exp(lam_re * dt)
    ar = mag * jnp.cos(lam_im * dt)
    ai = mag * jnp.sin(lam_im * dt)
    wr, wi = ar - 1.0, ai
    den = lam_re * lam_re + lam_im * lam_im
    cr = (wr * lam_re + wi * lam_im) / den
    ci = (wi * lam_re - wr * lam_im) / den
    bbr = cr[..., None] * b_re - ci[..., None] * b_im
    bbi = cr[..., None] * b_im + ci[..., None] * b_re
    gpb = MXU_DIM // SSM_GROUP
    nblk = g // gpb
    eye = jnp.eye(gpb, dtype=F32)

    def pack_in(m):
        m = m.reshape(nblk, gpb, p, SSM_GROUP)
        return jnp.einsum("agph,gk->aghkp", m, eye).reshape(nblk, gpb * SSM_GROUP, gpb * p)

    def pack_out(m):
        m = m.reshape(nblk, gpb, SSM_GROUP, p)
        return jnp.einsum("aghp,gk->agpkh", m, eye).reshape(nblk, gpb * p, gpb * SSM_GROUP)

    w_in = jnp.stack([pack_in(bbr), pack_in(bbi)], axis=1).astype(BF16)
    w_out = jnp.stack([pack_out(c_re), pack_out(-c_im)], axis=1).astype(BF16)
    a_re = jnp.broadcast_to(ar.reshape(1, g * p), (SUBLANES, g * p))
    a_im = jnp.broadcast_to(ai.reshape(1, g * p), (SUBLANES, g * p))
    return w_in, w_out, a_re, a_im


def _ssm_body(u_ref, win_ref, are_ref, aim_ref, wout_ref, dskip_ref, wglu_ref, bglu_ref, g_ref,
              o_ref, z_s, v_s, *, ts):
    nblk = win_ref.shape[0]
    blk_in = win_ref.shape[2]
    blk_state = win_ref.shape[3]
    n_state = nblk * blk_state

    @pl.when(pl.program_id(0) == 0)
    def _():
        z_s[...] = jnp.zeros(z_s.shape, F32)

    u = u_ref[...]
    ub = u.astype(BF16)
    for a in range(nblk):
        ua = ub[:, a * blk_in:(a + 1) * blk_in]
        for c in range(2):
            lo = c * n_state + a * blk_state
            v_s[:, lo:lo + blk_state] = jnp.dot(ua, win_ref[a, c], preferred_element_type=F32)

    def step(t, z):
        zr, zi = z
        row = pl.ds(pl.multiple_of(t * SUBLANES, SUBLANES), SUBLANES)
        ar = are_ref[...]
        ai = aim_ref[...]
        nzr = ar * zr - ai * zi + v_s[row, :n_state]
        nzi = ar * zi + ai * zr + v_s[row, n_state:]
        v_s[row, :n_state] = nzr
        v_s[row, n_state:] = nzi
        return nzr, nzi

    zr, zi = lax.fori_loop(0, ts, step, (z_s[:, :n_state], z_s[:, n_state:]))
    z_s[:, :n_state] = zr
    z_s[:, n_state:] = zi

    ys = []
    for a in range(nblk):
        lo = a * blk_state
        y = jnp.dot(v_s[:, lo:lo + blk_state].astype(BF16), wout_ref[a, 0], preferred_element_type=F32)
        y += jnp.dot(v_s[:, n_state + lo:n_state + lo + blk_state].astype(BF16), wout_ref[a, 1],
                     preferred_element_type=F32)
        ys.append(y)
    y = jnp.concatenate(ys, axis=1) + dskip_ref[...] * u
    y = jax.nn.gelu(y)
    gate = jnp.dot(y.astype(BF16), wglu_ref[...], preferred_element_type=F32) + bglu_ref[...]
    y = y * jax.nn.sigmoid(gate)
    o_ref[...] = _rms_norm(y, g_ref[...]).astype(o_ref.dtype)


def _ssm(u_tm, w_in, a_re, a_im, w_out, d_skip, w_glu, b_glu, norm_g, *, ts):
    rows, d_ssm = u_tm.shape
    tile = ts * SUBLANES
    n_lanes = 2 * a_re.shape[1]
    row = pl.BlockSpec((tile, d_ssm), lambda i: (i, 0))
    return pl.pallas_call(
        functools.partial(_ssm_body, ts=ts),
        grid=(rows // tile,),
        in_specs=[row, _const_spec(w_in.shape), _const_spec(a_re.shape), _const_spec(a_im.shape),
                  _const_spec(w_out.shape), _const_spec((1, d_ssm)), _const_spec((d_ssm, d_ssm)),
                  _const_spec((1, d_ssm)), _const_spec((1, d_ssm))],
        out_specs=row,
        out_shape=jax.ShapeDtypeStruct((rows, d_ssm), BF16),
        scratch_shapes=[pltpu.VMEM((SUBLANES, n_lanes), F32), pltpu.VMEM((tile, n_lanes), F32)],
        compiler_params=_params("arbitrary"),
        name="ssm",
    )(u_tm, w_in, a_re, a_im, w_out, d_skip, w_glu, b_glu, norm_g)


def _outproj_body(h_ref, attn_ref, ssm_ref, ga_ref, wa_ref, ws_ref, post_ref, o_ref):
    an = _rms_norm(attn_ref[...], ga_ref[...]).astype(BF16)
    mix = jnp.dot(an, wa_ref[...], preferred_element_type=F32)
    mix += jnp.dot(ssm_ref[...], ws_ref[...], preferred_element_type=F32)
    o_ref[...] = h_ref[...] + _rms_norm(mix, post_ref[...])


def _outproj(h, attn, ssm_n, attn_g, w_a, w_s, post_g, *, tm):
    t, d = h.shape
    da, ds = attn.shape[1], ssm_n.shape[1]
    row = lambda w: pl.BlockSpec((tm, w), lambda i: (i, 0))
    return pl.pallas_call(
        _outproj_body,
        grid=(t // tm,),
        in_specs=[row(d), row(da), row(ds), _const_spec((1, da)), _const_spec((da, d)),
                  _const_spec((ds, d)), _const_spec((1, d))],
        out_specs=row(d),
        out_shape=jax.ShapeDtypeStruct((t, d), F32),
        compiler_params=_params("parallel"),
        name="outproj",
    )(h, attn, ssm_n, attn_g, w_a, w_s, post_g)


def _ple_body(h_ref, p_ref, wup_ref, wgate_ref, post_ref, o_ref):
    h = h_ref[...]
    up = jnp.dot(p_ref[...].astype(BF16), wup_ref[...], preferred_element_type=F32)
    gate = jnp.dot(h.astype(BF16), wgate_ref[...], preferred_element_type=F32)
    o_ref[...] = h + _rms_norm(up * jax.nn.sigmoid(gate), post_ref[...])


def _ple(h, p, w_up, w_gate, post_g, *, tm):
    t, d = h.shape
    dp = p.shape[1]
    row = lambda w: pl.BlockSpec((tm, w), lambda i: (i, 0))
    return pl.pallas_call(
        _ple_body,
        grid=(t // tm,),
        in_specs=[row(d), row(dp), _const_spec((dp, d)), _const_spec((d, d)), _const_spec((1, d))],
        out_specs=row(d),
        out_shape=jax.ShapeDtypeStruct((t, d), F32),
        compiler_params=_params("parallel"),
        name="ple",
    )(h, p, w_up, w_gate, post_g)


TOKEN_TILE = 512
SSM_TIME_TILE = 64


def kernel(x, p, positions, ffn1_pre_g, ffn1_w_gate, ffn1_w_up, ffn1_w_down, ffn1_post_g, mix_pre_g, w_in, attn_norm_g, ssm_lam_re, ssm_lam_im, ssm_log_dt, ssm_b_re, ssm_b_im, ssm_c_re, ssm_c_im, ssm_d, ssm_w_glu, ssm_b_glu, ssm_norm_g, w_out, mix_post_g, ffn2_pre_g, ffn2_w_gate, ffn2_w_up, ffn2_w_down, ffn2_post_g, ple_w_up, ple_w_gate, ple_post_g):
    b, s, d = x.shape
    depth = p.shape[0]
    t = b * s
    d_attn = attn_norm_g.shape[1]
    d_ssm = ssm_norm_g.shape[1]
    assert b == SUBLANES, "the S5 scan keeps the batch on the sublane axis"
    tm = min(TOKEN_TILE, t)
    ts = min(SSM_TIME_TILE, s)
    vec = lambda a: a.reshape(1, -1)
    bf = lambda a: a.astype(BF16)

    h = x.reshape(t, d)
    pos = positions.reshape(t, 1).astype(F32)
    for i in range(depth):
        h = _ffn(h, vec(ffn1_pre_g[i]), bf(ffn1_w_gate[i]), bf(ffn1_w_up[i]), bf(ffn1_w_down[i]),
                 vec(ffn1_post_g[i]), tm=tm)

        q, k, v, u = _inproj(h, pos, vec(mix_pre_g[i]), bf(w_in[i]), d_attn=d_attn, tm=tm)
        attn = _attention(q.reshape(b, s, d_attn), k.reshape(b, s, d_attn), v.reshape(b, s, d_attn))
        ssm_w_in, ssm_w_out, a_re, a_im = _ssm_matrices(
            ssm_lam_re[i], ssm_lam_im[i], ssm_log_dt[i], ssm_b_re[i], ssm_b_im[i], ssm_c_re[i], ssm_c_im[i])
        u_tm = u.reshape(b, s, d_ssm).transpose(1, 0, 2).reshape(s * b, d_ssm)
        ssm_tm = _ssm(u_tm, ssm_w_in, a_re, a_im, ssm_w_out, vec(ssm_d[i]), bf(ssm_w_glu[i]),
                      vec(ssm_b_glu[i]), vec(ssm_norm_g[i]), ts=ts)
        ssm_n = ssm_tm.reshape(s, b, d_ssm).transpose(1, 0, 2).reshape(t, d_ssm)
        h = _outproj(h, attn.reshape(t, d_attn), ssm_n, vec(attn_norm_g[i]), bf(w_out[i][:d_attn]),
                     bf(w_out[i][d_attn:]), vec(mix_post_g[i]), tm=tm)

        h = _ffn(h, vec(ffn2_pre_g[i]), bf(ffn2_w_gate[i]), bf(ffn2_w_up[i]), bf(ffn2_w_down[i]),
                 vec(ffn2_post_g[i]), tm=tm)

        h = _ple(h, p[i].reshape(t, -1), bf(ple_w_up[i]), bf(ple_w_gate[i]), vec(ple_post_g[i]), tm=tm)
    return h.reshape(b, s, d)
```

```python
import functools
import math

import numpy as np
import jax
import jax.numpy as jnp
from jax import lax
from jax.experimental import pallas as pl
from jax.experimental.pallas import tpu as pltpu

F32 = jnp.float32
BF16 = jnp.bfloat16

LANES = 128
SUBLANES = 8
MXU_DIM = 256

HEAD_DIM = 64
ROPE_DIM = HEAD_DIM // 4
ROPE_THETA = 500000.0
DILATED_PATTERNS = ((128, 1), (512, 4), (2048, 16))
BAND = 128
SSM_GROUP = 16
SSM_STATE = 64
NORM_EPS = 1e-6
MASK_VALUE = -1e30

assert all(w // d == BAND for w, d in DILATED_PATTERNS)
DILATIONS = tuple(d for _, d in DILATED_PATTERNS)
ATTN_PLANES = 4
ATTN_UNROLL = 8

VMEM_LIMIT_BYTES = 56 * 1024 * 1024


def _rms_norm(x, g):
    ms = jnp.mean(x * x, axis=-1, keepdims=True)
    return x * lax.rsqrt(ms + NORM_EPS) * g


def _const_spec(shape):
    zeros = (0,) * len(shape)
    return pl.BlockSpec(shape, lambda *_: zeros, pipeline_mode=pl.Buffered(1))


def _params(*sem):
    return pltpu.CompilerParams(dimension_semantics=sem, vmem_limit_bytes=VMEM_LIMIT_BYTES)


def _ffn_body(x_ref, pre_ref, wg_ref, wu_ref, wd_ref, post_ref, o_ref, *, n_chunks):
    x = x_ref[...]
    xn = _rms_norm(x, pre_ref[...]).astype(BF16)
    ck = wg_ref.shape[1] // n_chunks
    acc = None
    for c in range(n_chunks):
        sl = slice(c * ck, (c + 1) * ck)
        g = jnp.dot(xn, wg_ref[:, sl], preferred_element_type=F32)
        u = jnp.dot(xn, wu_ref[:, sl], preferred_element_type=F32)
        mid = (g * jax.nn.sigmoid(g) * u).astype(BF16)
        part = jnp.dot(mid, wd_ref[sl, :], preferred_element_type=F32)
        acc = part if acc is None else acc + part
    o_ref[...] = x + 0.5 * _rms_norm(acc, post_ref[...])


def _ffn(h, pre_g, wg, wu, wd, post_g, *, tm):
    t, d = h.shape
    ff = wg.shape[1]
    n_chunks = 2 if ff % (2 * LANES) == 0 else 1
    row = pl.BlockSpec((tm, d), lambda i: (i, 0))
    return pl.pallas_call(
        functools.partial(_ffn_body, n_chunks=n_chunks),
        grid=(t // tm,),
        in_specs=[row, _const_spec((1, d)), _const_spec((d, ff)), _const_spec((d, ff)),
                  _const_spec((ff, d)), _const_spec((1, d))],
        out_specs=row,
        out_shape=jax.ShapeDtypeStruct((t, d), F32),
        compiler_params=_params("parallel"),
        name="ffn",
    )(h, pre_g, wg, wu, wd, post_g)


def _rope_tables():
    half = ROPE_DIM // 2
    inv_freq = ROPE_THETA ** (-jnp.arange(half, dtype=F32) * (2.0 / ROPE_DIM))
    lane = np.arange(LANES)
    in_head = lane % HEAD_DIM
    freq = jnp.where(in_head < ROPE_DIM, inv_freq[lane % half], 0.0).astype(F32)
    first = (in_head < half).astype(np.float32)
    second = ((in_head >= half) & (in_head < ROPE_DIM)).astype(np.float32)
    return freq.reshape(1, LANES), jnp.asarray(first).reshape(1, LANES), jnp.asarray(second).reshape(1, LANES)


def _inproj_body(x_ref, pos_ref, g_ref, w_ref, freq_ref, first_ref, second_ref,
                 q_ref, k_ref, v_ref, u_ref, *, d_attn):
    xn = _rms_norm(x_ref[...], g_ref[...]).astype(BF16)
    proj = jnp.dot(xn, w_ref[...], preferred_element_type=F32)
    ang = pos_ref[...] * freq_ref[...]
    cos = jnp.cos(ang)
    sin = jnp.sin(ang)
    half = ROPE_DIM // 2
    sin_first = -sin * first_ref[...]
    sin_second = sin * second_ref[...]

    def rotary(t):
        return (t * cos + pltpu.roll(t, LANES - half, 1) * sin_first
                + pltpu.roll(t, half, 1) * sin_second)

    scale = HEAD_DIM ** -0.5
    for j in range(d_attn // LANES):
        sl = slice(j * LANES, (j + 1) * LANES)
        q_ref[:, sl] = rotary(proj[:, j * LANES:(j + 1) * LANES]) * scale
        k_ref[:, sl] = rotary(proj[:, d_attn + j * LANES:d_attn + (j + 1) * LANES])
    v_ref[...] = proj[:, 2 * d_attn:3 * d_attn]
    u_ref[...] = proj[:, 3 * d_attn:]


def _inproj(h, pos, g, w_in, *, d_attn, tm):
    t, d = h.shape
    n = w_in.shape[1]
    d_ssm = n - 3 * d_attn
    freq, first, second = _rope_tables()
    row = lambda w: pl.BlockSpec((tm, w), lambda i: (i, 0))
    return pl.pallas_call(
        functools.partial(_inproj_body, d_attn=d_attn),
        grid=(t // tm,),
        in_specs=[row(d), row(1), _const_spec((1, d)), _const_spec((d, n)),
                  _const_spec((1, LANES)), _const_spec((1, LANES)), _const_spec((1, LANES))],
        out_specs=[row(d_attn), row(d_attn), row(d_attn), row(d_ssm)],
        out_shape=[jax.ShapeDtypeStruct((t, d_attn), F32)] * 3 + [jax.ShapeDtypeStruct((t, d_ssm), F32)],
        compiler_params=_params("parallel"),
        name="inproj",
    )(h, pos, g, w_in, freq, first, second)


def _attn_body(q_ref, k_ref, v_ref, o_ref, planes_s, m_s, l_s, acc_s, *, seq):
    lane = lax.broadcasted_iota(jnp.int32, (BAND, LANES), 1)
    head0 = lane < HEAD_DIM
    qi = lax.broadcasted_iota(jnp.int32, (BAND, BAND), 0)
    kj = lax.broadcasted_iota(jnp.int32, (BAND, BAND), 1)
    cur_ok = kj <= qi
    prev_ok = kj >= qi

    plane_rows = seq // ATTN_PLANES
    ones = jnp.ones((2 * BAND, LANES), BF16)

    for t, ref in enumerate((q_ref, k_ref, v_ref)):
        for r in range(ATTN_PLANES):
            planes_s[t, pl.ds(r * plane_rows, plane_rows), :] = ref[pl.ds(r, plane_rows, stride=ATTN_PLANES), :]

    for pi, d in enumerate(DILATIONS):
        nb = seq // (BAND * d)
        step = max(d // ATTN_PLANES, 1)

        def rows(start):
            return pl.ds(start, BAND) if step == 1 else pl.ds(start, BAND, stride=step)

        def load(t, start):
            return (q_ref, k_ref, v_ref)[t][rows(start), :] if d == 1 else planes_s[t, rows(start), :]

        def block(n, carry):
            r = n // nb
            i = n - r * nb
            base = r if d == 1 else (r % ATTN_PLANES) * plane_rows + r // ATTN_PLANES
            start = base + i * (BAND * step)
            prev = jnp.maximum(start - BAND * step, base)
            q = load(0, start)
            zero = jnp.zeros_like(q)
            q2 = jnp.concatenate([jnp.where(head0, q, zero), jnp.where(head0, zero, q)], axis=0)
            kk = jnp.concatenate([load(1, prev), load(1, start)], axis=0)
            vv = jnp.concatenate([load(2, prev), load(2, start)], axis=0).astype(BF16)
            s = lax.dot_general(q2.astype(BF16), kk.astype(BF16), (((1,), (1,)), ((), ())),
                                preferred_element_type=F32)
            ok = jnp.concatenate([prev_ok & (i > 0), cur_ok], axis=1)
            ok = jnp.concatenate([ok, ok], axis=0)
            s = jnp.where(ok, s, MASK_VALUE)
            m = jnp.max(s, axis=-1, keepdims=True)
            e = jnp.exp(s - m).astype(BF16)
            o2 = jnp.dot(e, jnp.concatenate([vv, ones], axis=1), preferred_element_type=F32)
            m_s[pi, rows(start), :] = jnp.where(head0, m[:BAND], m[BAND:])
            l_s[pi, rows(start), :] = jnp.where(head0, o2[:BAND, LANES:], o2[BAND:, LANES:])
            acc_s[pi, rows(start), :] = jnp.where(head0, o2[:BAND, :LANES], o2[BAND:, :LANES])
            return carry

        lax.fori_loop(0, d * nb, block, 0, unroll=ATTN_UNROLL)

    blocks_per_plane = plane_rows // BAND

    def merge(n, carry):
        r = n // blocks_per_plane
        j = n - r * blocks_per_plane
        orig = pl.ds(r + j * (BAND * ATTN_PLANES), BAND, stride=ATTN_PLANES)
        plane = pl.ds(r * plane_rows + j * BAND, BAND)
        sel = lambda ref: [ref[0, orig, :]] + [ref[pi, plane, :] for pi in range(1, len(DILATIONS))]
        ms, ls, accs = sel(m_s), sel(l_s), sel(acc_s)
        m = functools.reduce(jnp.maximum, ms)
        ws = [jnp.exp(mp - m) for mp in ms]
        num = sum(w * a for w, a in zip(ws, accs))
        den = sum(w * l for w, l in zip(ws, ls))
        o_ref[orig, :] = num / den
        return carry

    lax.fori_loop(0, seq // BAND, merge, 0, unroll=2)


def _attention(q, k, v):
    b, s, d_attn = q.shape
    assert s % (BAND * max(DILATIONS)) == 0 and d_attn % LANES == 0
    assert all(d == 1 or d % ATTN_PLANES == 0 for d in DILATIONS)
    spec = pl.BlockSpec((None, s, LANES), lambda bi, hi: (bi, 0, hi))
    stats = pltpu.VMEM((len(DILATIONS), s, LANES), F32)
    return pl.pallas_call(
        functools.partial(_attn_body, seq=s),
        grid=(b, d_attn // LANES),
        in_specs=[spec, spec, spec],
        out_specs=spec,
        out_shape=jax.ShapeDtypeStruct((b, s, d_attn), F32),
        scratch_shapes=[pltpu.VMEM((3, s, LANES), F32), stats, stats, stats],
        compiler_params=_params("parallel", "parallel"),
        name="attention",
    )(q, k, v)


def _ssm_matrices(lam_re, lam_im, log_dt, b_re, b_im, c_re, c_im):
    g, p = lam_re.shape
    dt = jnp.exp(log_dt)[:, None]
    mag = jnp.exp(lam_re * dt)
    ar = mag * jnp.cos(lam_im * dt)
    ai = mag * jnp.sin(lam_im * dt)
    wr, wi = ar - 1.0, ai
    den = lam_re * lam_re + lam_im * lam_im
    cr = (wr * lam_re + wi * lam_im) / den
    ci = (wi * lam_re - wr * lam_im) / den
    bbr = cr[..., None] * b_re - ci[..., None] * b_im
    bbi = cr[..., None] * b_im + ci[..., None] * b_re
    gpb = MXU_DIM // SSM_GROUP
    nblk = g // gpb
    eye = jnp.eye(gpb, dtype=F32)

    def pack_in(m):
        m = m.reshape(nblk, gpb, p, SSM_GROUP)
        return jnp.einsum("agph,gk->aghkp", m, eye).reshape(nblk, gpb * SSM_GROUP, gpb * p)

    def pack_out(m):
        m = m.reshape(nblk, gpb, SSM_GROUP, p)
        return jnp.einsum("aghp,gk->agpkh", m, eye).reshape(nblk, gpb * p, gpb * SSM_GROUP)

    w_in = jnp.stack([pack_in(bbr), pack_in(bbi)], axis=1).astype(BF16)
    w_out = jnp.stack([pack_out(c_re), pack_out(-c_im)], axis=1).astype(BF16)
    a_re = jnp.broadcast_to(ar.reshape(1, g * p), (SUBLANES, g * p))
    a_im = jnp.broadcast_to(ai.reshape(1, g * p), (SUBLANES, g * p))
    return w_in, w_out, a_re, a_im


def _ssm_body(u_ref, win_ref, are_ref, aim_ref, wout_ref, dskip_ref, wglu_ref, bglu_ref, g_ref,
              o_ref, z_s, v_s, *, ts):
    nblk = win_ref.shape[0]
    blk_in = win_ref.shape[2]
    blk_state = win_ref.shape[3]
    n_state = nblk * blk_state

    @pl.when(pl.program_id(0) == 0)
    def _():
        z_s[...] = jnp.zeros(z_s.shape, F32)

    u = u_ref[...]
    ub = u.astype(BF16)
    for a in range(nblk):
        ua = ub[:, a * blk_in:(a + 1) * blk_in]
        for c in range(2):
            lo = c * n_state + a * blk_state
            v_s[:, lo:lo + blk_state] = jnp.dot(ua, win_ref[a, c], preferred_element_type=F32)

    def step(t, z):
        zr, zi = z
        row = pl.ds(pl.multiple_of(t * SUBLANES, SUBLANES), SUBLANES)
        ar = are_ref[...]
        ai = aim_ref[...]
        nzr = ar * zr - ai * zi + v_s[row, :n_state]
        nzi = ar * zi + ai * zr + v_s[row, n_state:]
        v_s[row, :n_state] = nzr
        v_s[row, n_state:] = nzi
        return nzr, nzi

    zr, zi = lax.fori_loop(0, ts, step, (z_s[:, :n_state], z_s[:, n_state:]))
    z_s[:, :n_state] = zr
    z_s[:, n_state:] = zi

    ys = []
    for a in range(nblk):
        lo = a * blk_state
        y = jnp.dot(v_s[:, lo:lo + blk_state].astype(BF16), wout_ref[a, 0], preferred_element_type=F32)
        y += jnp.dot(v_s[:, n_state + lo:n_state + lo + blk_state].astype(BF16), wout_ref[a, 1],
                     preferred_element_type=F32)
        ys.append(y)
    y = jnp.concatenate(ys, axis=1) + dskip_ref[...] * u
    y = jax.nn.gelu(y)
    gate = jnp.dot(y.astype(BF16), wglu_ref[...], preferred_element_type=F32) + bglu_ref[...]
    y = y * jax.nn.sigmoid(gate)
    o_ref[...] = _rms_norm(y, g_ref[...]).astype(o_ref.dtype)


def _ssm(u_tm, w_in, a_re, a_im, w_out, d_skip, w_glu, b_glu, norm_g, *, ts):
    rows, d_ssm = u_tm.shape
    tile = ts * SUBLANES
    n_lanes = 2 * a_re.shape[1]
    row = pl.BlockSpec((tile, d_ssm), lambda i: (i, 0))
    return pl.pallas_call(
        functools.partial(_ssm_body, ts=ts),
        grid=(rows // tile,),
        in_specs=[row, _const_spec(w_in.shape), _const_spec(a_re.shape), _const_spec(a_im.shape),
                  _const_spec(w_out.shape), _const_spec((1, d_ssm)), _const_spec((d_ssm, d_ssm)),
                  _const_spec((1, d_ssm)), _const_spec((1, d_ssm))],
        out_specs=row,
        out_shape=jax.ShapeDtypeStruct((rows, d_ssm), BF16),
        scratch_shapes=[pltpu.VMEM((SUBLANES, n_lanes), F32), pltpu.VMEM((tile, n_lanes), F32)],
        compiler_params=_params("arbitrary"),
        name="ssm",
    )(u_tm, w_in, a_re, a_im, w_out, d_skip, w_glu, b_glu, norm_g)


def _outproj_body(h_ref, attn_ref, ssm_ref, ga_ref, wa_ref, ws_ref, post_ref, o_ref):
    an = _rms_norm(attn_ref[...], ga_ref[...]).astype(BF16)
    mix = jnp.dot(an, wa_ref[...], preferred_element_type=F32)
    mix += jnp.dot(ssm_ref[...], ws_ref[...], preferred_element_type=F32)
    o_ref[...] = h_ref[...] + _rms_norm(mix, post_ref[...])


def _outproj(h, attn, ssm_n, attn_g, w_a, w_s, post_g, *, tm):
    t, d = h.shape
    da, ds = attn.shape[1], ssm_n.shape[1]
    row = lambda w: pl.BlockSpec((tm, w), lambda i: (i, 0))
    return pl.pallas_call(
        _outproj_body,
        grid=(t // tm,),
        in_specs=[row(d), row(da), row(ds), _const_spec((1, da)), _const_spec((da, d)),
                  _const_spec((ds, d)), _const_spec((1, d))],
        out_specs=row(d),
        out_shape=jax.ShapeDtypeStruct((t, d), F32),
        compiler_params=_params("parallel"),
        name="outproj",
    )(h, attn, ssm_n, attn_g, w_a, w_s, post_g)


def _ple_body(h_ref, p_ref, wup_ref, wgate_ref, post_ref, o_ref):
    h = h_ref[...]
    up = jnp.dot(p_ref[...].astype(BF16), wup_ref[...], preferred_element_type=F32)
    gate = jnp.dot(h.astype(BF16), wgate_ref[...], preferred_element_type=F32)
    o_ref[...] = h + _rms_norm(up * jax.nn.sigmoid(gate), post_ref[...])


def _ple(h, p, w_up, w_gate, post_g, *, tm):
    t, d = h.shape
    dp = p.shape[1]
    row = lambda w: pl.BlockSpec((tm, w), lambda i: (i, 0))
    return pl.pallas_call(
        _ple_body,
        grid=(t // tm,),
        in_specs=[row(d), row(dp), _const_spec((dp, d)), _const_spec((d, d)), _const_spec((1, d))],
        out_specs=row(d),
        out_shape=jax.ShapeDtypeStruct((t, d), F32),
        compiler_params=_params("parallel"),
        name="ple",
    )(h, p, w_up, w_gate, post_g)


TOKEN_TILE = 512
SSM_TIME_TILE = 64


def kernel(x, p, positions, ffn1_pre_g, ffn1_w_gate, ffn1_w_up, ffn1_w_down, ffn1_post_g, mix_pre_g, w_in, attn_norm_g, ssm_lam_re, ssm_lam_im, ssm_log_dt, ssm_b_re, ssm_b_im, ssm_c_re, ssm_c_im, ssm_d, ssm_w_glu, ssm_b_glu, ssm_norm_g, w_out, mix_post_g, ffn2_pre_g, ffn2_w_gate, ffn2_w_up, ffn2_w_down, ffn2_post_g, ple_w_up, ple_w_gate, ple_post_g):
    b, s, d = x.shape
    depth = p.shape[0]
    t = b * s
    d_attn = attn_norm_g.shape[1]
    d_ssm = ssm_norm_g.shape[1]
    assert b == SUBLANES, "the S5 scan keeps the batch on the sublane axis"
    tm = min(TOKEN_TILE, t)
    ts = min(SSM_TIME_TILE, s)
    vec = lambda a: a.reshape(1, -1)
    bf = lambda a: a.astype(BF16)

    h = x.reshape(t, d)
    pos = positions.reshape(t, 1).astype(F32)
    for i in range(depth):
        h = _ffn(h, vec(ffn1_pre_g[i]), bf(ffn1_w_gate[i]), bf(ffn1_w_up[i]), bf(ffn1_w_down[i]),
                 vec(ffn1_post_g[i]), tm=tm)

        q, k, v, u = _inproj(h, pos, vec(mix_pre_g[i]), bf(w_in[i]), d_attn=d_attn, tm=tm)
        attn = _attention(q.reshape(b, s, d_attn), k.reshape(b, s, d_attn), v.reshape(b, s, d_attn))
        ssm_w_in, ssm_w_out, a_re, a_im = _ssm_matrices(
            ssm_lam_re[i], ssm_lam_im[i], ssm_log_dt[i], ssm_b_re[i], ssm_b_im[i], ssm_c_re[i], ssm_c_im[i])
        u_tm = u.reshape(b, s, d_ssm).transpose(1, 0, 2).reshape(s * b, d_ssm)
        ssm_tm = _ssm(u_tm, ssm_w_in, a_re, a_im, ssm_w_out, vec(ssm_d[i]), bf(ssm_w_glu[i]),
                      vec(ssm_b_glu[i]), vec(ssm_norm_g[i]), ts=ts)
        ssm_n = ssm_tm.reshape(s, b, d_ssm).transpose(1, 0, 2).reshape(t, d_ssm)
        h = _outproj(h, attn.reshape(t, d_attn), ssm_n, vec(attn_norm_g[i]), bf(w_out[i][:d_attn]),
                     bf(w_out[i][d_attn:]), vec(mix_post_g[i]), tm=tm)

        h = _ffn(h, vec(ffn2_pre_g[i]), bf(ffn2_w_gate[i]), bf(ffn2_w_up[i]), bf(ffn2_w_down[i]),
                 vec(ffn2_post_g[i]), tm=tm)

        h = _ple(h, p[i].reshape(t, -1), bf(ple_w_up[i]), bf(ple_w_gate[i]), vec(ple_post_g[i]), tm=tm)
    return h.reshape(b, s, d)
```

```python
import functools

import numpy as np
import jax
import jax.numpy as jnp
from jax import lax
from jax.experimental import pallas as pl
from jax.experimental.pallas import tpu as pltpu

F32 = jnp.float32
BF16 = jnp.bfloat16

LANES = 128
SUBLANES = 8
MXU_DIM = 256

HEAD_DIM = 64
ROPE_DIM = HEAD_DIM // 4
ROPE_THETA = 500000.0
DILATED_PATTERNS = ((128, 1), (512, 4), (2048, 16))
BAND = 128
SSM_GROUP = 16
SSM_STATE = 64
NORM_EPS = 1e-6
MASK_VALUE = -1e30

assert all(w // d == BAND for w, d in DILATED_PATTERNS)
DILATIONS = tuple(d for _, d in DILATED_PATTERNS)
ATTN_PLANES = 4
ATTN_UNROLL = 8

TOKEN_TILE = 512
FFN_CHUNKS = 1
SSM_TIME_TILE = 128
VMEM_LIMIT_BYTES = 56 * 1024 * 1024


def _rms_norm(x, g):
    ms = jnp.mean(x * x, axis=-1, keepdims=True)
    return x * lax.rsqrt(ms + NORM_EPS) * g


def _const_spec(shape):
    zeros = (0,) * len(shape)
    return pl.BlockSpec(shape, lambda *_: zeros, pipeline_mode=pl.Buffered(1))


def _params(*sem):
    return pltpu.CompilerParams(dimension_semantics=sem, vmem_limit_bytes=VMEM_LIMIT_BYTES)


def _ffn_chunks(ff):
    tiles = -(-ff // MXU_DIM)
    cuts = [min(ff, (tiles * c // FFN_CHUNKS) * MXU_DIM) for c in range(FFN_CHUNKS + 1)]
    return [(lo, hi) for lo, hi in zip(cuts[:-1], cuts[1:]) if hi > lo]


def _ffn_math(x, pre_ref, wg_ref, wu_ref, wd_ref, post_ref):
    xn = _rms_norm(x, pre_ref[...]).astype(BF16)
    acc = None
    for lo, hi in _ffn_chunks(wg_ref.shape[1]):
        g = jnp.dot(xn, wg_ref[:, lo:hi], preferred_element_type=F32)
        u = jnp.dot(xn, wu_ref[:, lo:hi], preferred_element_type=F32)
        mid = (g * jax.nn.sigmoid(g) * u).astype(BF16)
        part = jnp.dot(mid, wd_ref[lo:hi, :], preferred_element_type=F32)
        acc = part if acc is None else acc + part
    return x + 0.5 * _rms_norm(acc, post_ref[...])


def _rope_tables():
    half = ROPE_DIM // 2
    inv_freq = ROPE_THETA ** (-jnp.arange(half, dtype=F32) * (2.0 / ROPE_DIM))
    lane = np.arange(LANES)
    in_head = lane % HEAD_DIM
    freq = jnp.where(in_head < ROPE_DIM, inv_freq[lane % half], 0.0).astype(F32)
    first = (in_head < half).astype(np.float32)
    second = ((in_head >= half) & (in_head < ROPE_DIM)).astype(np.float32)
    return freq.reshape(1, LANES), jnp.asarray(first).reshape(1, LANES), jnp.asarray(second).reshape(1, LANES)


def _pre_body(x_ref, pos_ref, pre_ref, wg_ref, wu_ref, wd_ref, post_ref, g_ref, w_ref,
              freq_ref, first_ref, second_ref, h_ref, q_ref, k_ref, v_ref, u_ref, *, d_attn):
    h = _ffn_math(x_ref[...], pre_ref, wg_ref, wu_ref, wd_ref, post_ref)
    h_ref[...] = h
    xn = _rms_norm(h, g_ref[...]).astype(BF16)
    proj = jnp.dot(xn, w_ref[...], preferred_element_type=F32)
    ang = pos_ref[...] * freq_ref[...]
    cos = jnp.cos(ang)
    sin = jnp.sin(ang)
    half = ROPE_DIM // 2
    sin_first = -sin * first_ref[...]
    sin_second = sin * second_ref[...]

    def rotary(t):
        return (t * cos + pltpu.roll(t, LANES - half, 1) * sin_first
                + pltpu.roll(t, half, 1) * sin_second)

    scale = HEAD_DIM ** -0.5
    for j in range(d_attn // LANES):
        sl = slice(j * LANES, (j + 1) * LANES)
        q_ref[:, sl] = rotary(proj[:, j * LANES:(j + 1) * LANES]) * scale
        k_ref[:, sl] = rotary(proj[:, d_attn + j * LANES:d_attn + (j + 1) * LANES])
    v_ref[...] = proj[:, 2 * d_attn:3 * d_attn]
    u_ref[...] = proj[:, 3 * d_attn:]


def _pre(h, pos, pre_g, wg, wu, wd, post_g, mix_g, w_in, *, d_attn, tm):
    t, d = h.shape
    ff = wg.shape[1]
    n = w_in.shape[1]
    d_ssm = n - 3 * d_attn
    freq, first, second = _rope_tables()
    row = lambda w: pl.BlockSpec((tm, w), lambda i: (i, 0))
    lane_tab = _const_spec((1, LANES))
    return pl.pallas_call(
        functools.partial(_pre_body, d_attn=d_attn),
        grid=(t // tm,),
        in_specs=[row(d), row(1), _const_spec((1, d)), _const_spec((d, ff)), _const_spec((d, ff)),
                  _const_spec((ff, d)), _const_spec((1, d)), _const_spec((1, d)), _const_spec((d, n)),
                  lane_tab, lane_tab, lane_tab],
        out_specs=[row(d), row(d_attn), row(d_attn), row(d_attn), row(d_ssm)],
        out_shape=[jax.ShapeDtypeStruct((t, d), F32)] + [jax.ShapeDtypeStruct((t, d_attn), F32)] * 3
        + [jax.ShapeDtypeStruct((t, d_ssm), F32)],
        compiler_params=_params("parallel"),
        name="pre",
    )(h, pos, pre_g, wg, wu, wd, post_g, mix_g, w_in, freq, first, second)


def _attn_body(q_ref, k_ref, v_ref, o_ref, planes_s, m_s, l_s, acc_s, *, seq):
    lane = lax.broadcasted_iota(jnp.int32, (BAND, LANES), 1)
    head0 = lane < HEAD_DIM
    qi = lax.broadcasted_iota(jnp.int32, (BAND, BAND), 0)
    kj = lax.broadcasted_iota(jnp.int32, (BAND, BAND), 1)
    cur_ok = kj <= qi
    prev_ok = kj >= qi

    plane_rows = seq // ATTN_PLANES
    ones = jnp.ones((2 * BAND, LANES), BF16)

    for t, ref in enumerate((q_ref, k_ref, v_ref)):
        for r in range(ATTN_PLANES):
            planes_s[t, pl.ds(r * plane_rows, plane_rows), :] = ref[pl.ds(r, plane_rows, stride=ATTN_PLANES), :]

    for pi, d in enumerate(DILATIONS):
        nb = seq // (BAND * d)
        step = max(d // ATTN_PLANES, 1)

        def rows(start):
            return pl.ds(start, BAND) if step == 1 else pl.ds(start, BAND, stride=step)

        def load(t, start):
            return (q_ref, k_ref, v_ref)[t][rows(start), :] if d == 1 else planes_s[t, rows(start), :]

        def block(n, carry):
            r = n // nb
            i = n - r * nb
            base = r if d == 1 else (r % ATTN_PLANES) * plane_rows + r // ATTN_PLANES
            start = base + i * (BAND * step)
            prev = jnp.maximum(start - BAND * step, base)
            q = load(0, start)
            zero = jnp.zeros_like(q)
            q2 = jnp.concatenate([jnp.where(head0, q, zero), jnp.where(head0, zero, q)], axis=0)
            kk = jnp.concatenate([load(1, prev), load(1, start)], axis=0)
            vv = jnp.concatenate([load(2, prev), load(2, start)], axis=0).astype(BF16)
            s = lax.dot_general(q2.astype(BF16), kk.astype(BF16), (((1,), (1,)), ((), ())),
                                preferred_element_type=F32)
            ok = jnp.concatenate([prev_ok & (i > 0), cur_ok], axis=1)
            ok = jnp.concatenate([ok, ok], axis=0)
            s = jnp.where(ok, s, MASK_VALUE)
            m = jnp.max(s, axis=-1, keepdims=True)
            e = jnp.exp(s - m).astype(BF16)
            o2 = jnp.dot(e, jnp.concatenate([vv, ones], axis=1), preferred_element_type=F32)
            m_s[pi, rows(start), :] = jnp.where(head0, m[:BAND], m[BAND:])
            l_s[pi, rows(start), :] = jnp.where(head0, o2[:BAND, LANES:], o2[BAND:, LANES:])
            acc_s[pi, rows(start), :] = jnp.where(head0, o2[:BAND, :LANES], o2[BAND:, :LANES])
            return carry

        lax.fori_loop(0, d * nb, block, 0, unroll=ATTN_UNROLL)

    blocks_per_plane = plane_rows // BAND

    def merge(n, carry):
        r = n // blocks_per_plane
        j = n - r * blocks_per_plane
        orig = pl.ds(r + j * (BAND * ATTN_PLANES), BAND, stride=ATTN_PLANES)
        plane = pl.ds(r * plane_rows + j * BAND, BAND)
        sel = lambda ref: [ref[0, orig, :]] + [ref[pi, plane, :] for pi in range(1, len(DILATIONS))]
        ms, ls, accs = sel(m_s), sel(l_s), sel(acc_s)
        m = functools.reduce(jnp.maximum, ms)
        ws = [jnp.exp(mp - m) for mp in ms]
        num = sum(w * a for w, a in zip(ws, accs))
        den = sum(w * l for w, l in zip(ws, ls))
        o_ref[orig, :] = num / den
        return carry

    lax.fori_loop(0, seq // BAND, merge, 0, unroll=2)


def _attention(q, k, v):
    b, s, d_attn = q.shape
    assert s % (BAND * max(DILATIONS)) == 0 and d_attn % LANES == 0
    assert all(d == 1 or d % ATTN_PLANES == 0 for d in DILATIONS)
    spec = pl.BlockSpec((None, s, LANES), lambda bi, hi: (bi, 0, hi))
    stats = pltpu.VMEM((len(DILATIONS), s, LANES), F32)
    return pl.pallas_call(
        functools.partial(_attn_body, seq=s),
        grid=(b, d_attn // LANES),
        in_specs=[spec, spec, spec],
        out_specs=spec,
        out_shape=jax.ShapeDtypeStruct((b, s, d_attn), F32),
        scratch_shapes=[pltpu.VMEM((3, s, LANES), F32), stats, stats, stats],
        compiler_params=_params("parallel", "parallel"),
        name="attention",
    )(q, k, v)


def _ssm_matrices(lam_re, lam_im, log_dt, b_re, b_im, c_re, c_im):
    g, p = lam_re.shape
    dt = jnp.exp(log_dt)[:, None]
    mag = jnp.exp(lam_re * dt)
    ar = mag * jnp.cos(lam_im * dt)
    ai = mag * jnp.sin(lam_im * dt)
    wr, wi = ar - 1.0, ai
    den = lam_re * lam_re + lam_im * lam_im
    cr = (wr * lam_re + wi * lam_im) / den
    ci = (wi * lam_re - wr * lam_im) / den
    bbr = cr[..., None] * b_re - ci[..., None] * b_im
    bbi = cr[..., None] * b_im + ci[..., None] * b_re
    gpb = MXU_DIM // SSM_GROUP
    nblk = g // gpb
    eye = jnp.eye(gpb, dtype=F32)

    def pack_in(m):
        m = m.reshape(nblk, gpb, p, SSM_GROUP)
        return jnp.einsum("agph,gk->aghkp", m, eye).reshape(nblk, gpb * SSM_GROUP, gpb * p)

    def pack_out(m):
        m = m.reshape(nblk, gpb, SSM_GROUP, p)
        return jnp.einsum("aghp,gk->agpkh", m, eye).reshape(nblk, gpb * p, gpb * SSM_GROUP)

    w_in = jnp.stack([pack_in(bbr), pack_in(bbi)], axis=1).astype(BF16)
    w_out = jnp.stack([pack_out(c_re), pack_out(-c_im)], axis=1).astype(BF16)
    a_re = jnp.broadcast_to(ar.reshape(1, g * p), (SUBLANES, g * p))
    a_im = jnp.broadcast_to(ai.reshape(1, g * p), (SUBLANES, g * p))
    return w_in, w_out, a_re, a_im


def _ssm_body(u_ref, win_ref, are_ref, aim_ref, wout_ref, dskip_ref, wglu_ref, bglu_ref, g_ref,
              o_ref, z_s, v_s, io_s, *, ts):
    nb = u_ref.shape[0]
    nblk = win_ref.shape[0]
    blk_in = win_ref.shape[2]
    blk_state = win_ref.shape[3]
    n_state = nblk * blk_state
    io_slabs = io_s.shape[0]

    @pl.when(pl.program_id(0) == 0)
    def _():
        z_s[...] = jnp.zeros(z_s.shape, F32)

    def batch_rows(bi):
        return pl.ds(bi, ts, stride=nb)

    for bi in range(nb):
        for j in range(io_slabs):
            io_s[j, batch_rows(bi), :] = u_ref[bi, :, j * LANES:(j + 1) * LANES]
    u = jnp.concatenate([io_s[j] for j in range(io_slabs)], axis=1)
    ub = u.astype(BF16)
    for a in range(nblk):
        ua = ub[:, a * blk_in:(a + 1) * blk_in]
        for c in range(2):
            lo = c * n_state + a * blk_state
            v_s[:, lo:lo + blk_state] = jnp.dot(ua, win_ref[a, c], preferred_element_type=F32)

    for j in range(n_state // LANES):
        re = slice(j * LANES, (j + 1) * LANES)
        im = slice(n_state + j * LANES, n_state + (j + 1) * LANES)
        ar = are_ref[:, re]
        ai = aim_ref[:, re]
        zr = z_s[:, re]
        zi = z_s[:, im]
        for t in range(ts):
            row = slice(t * nb, (t + 1) * nb)
            nzr = ar * zr - ai * zi + v_s[row, re]
            nzi = ar * zi + ai * zr + v_s[row, im]
            v_s[row, re] = nzr
            v_s[row, im] = nzi
            zr, zi = nzr, nzi
        z_s[:, re] = zr
        z_s[:, im] = zi

    ys = []
    for a in range(nblk):
        lo = a * blk_state
        y = jnp.dot(v_s[:, lo:lo + blk_state].astype(BF16), wout_ref[a, 0], preferred_element_type=F32)
        y += jnp.dot(v_s[:, n_state + lo:n_state + lo + blk_state].astype(BF16), wout_ref[a, 1],
                     preferred_element_type=F32)
        ys.append(y)
    y = jnp.concatenate(ys, axis=1) + dskip_ref[...] * u
    y = jax.nn.gelu(y)
    gate = jnp.dot(y.astype(BF16), wglu_ref[...], preferred_element_type=F32) + bglu_ref[...]
    y = _rms_norm(y * jax.nn.sigmoid(gate), g_ref[...])
    for j in range(io_slabs):
        io_s[j] = y[:, j * LANES:(j + 1) * LANES]
    for bi in range(nb):
        o_ref[bi] = jnp.concatenate([io_s[j, batch_rows(bi), :] for j in range(io_slabs)],
                                    axis=1).astype(o_ref.dtype)


def _ssm(u, w_in, a_re, a_im, w_out, d_skip, w_glu, b_glu, norm_g, *, ts):
    b, s, d_ssm = u.shape
    assert b == SUBLANES, "the S5 scan keeps the batch on the sublane axis"
    n_state = a_re.shape[1]
    tile = pl.BlockSpec((b, ts, d_ssm), lambda i: (0, i, 0))
    return pl.pallas_call(
        functools.partial(_ssm_body, ts=ts),
        grid=(s // ts,),
        in_specs=[tile, _const_spec(w_in.shape), _const_spec(a_re.shape), _const_spec(a_im.shape),
                  _const_spec(w_out.shape), _const_spec((1, d_ssm)), _const_spec((d_ssm, d_ssm)),
                  _const_spec((1, d_ssm)), _const_spec((1, d_ssm))],
        out_specs=tile,
        out_shape=jax.ShapeDtypeStruct((b, s, d_ssm), BF16),
        scratch_shapes=[pltpu.VMEM((b, 2 * n_state), F32),
                        pltpu.VMEM((ts * b, 2 * n_state), F32),
                        pltpu.VMEM((d_ssm // LANES, ts * b, LANES), F32)],
        compiler_params=_params("arbitrary"),
        name="ssm",
    )(u, w_in, a_re, a_im, w_out, d_skip, w_glu, b_glu, norm_g)


def _post_body(h_ref, attn_ref, ssm_ref, p_ref, ga_ref, wa_ref, ws_ref, mixpost_ref,
               pre_ref, wg_ref, wu_ref, wd_ref, post_ref, wup_ref, wgate_ref, plepost_ref, o_ref):
    an = _rms_norm(attn_ref[...], ga_ref[...]).astype(BF16)
    mix = jnp.dot(an, wa_ref[...], preferred_element_type=F32)
    mix += jnp.dot(ssm_ref[...], ws_ref[...], preferred_element_type=F32)
    h = h_ref[...] + _rms_norm(mix, mixpost_ref[...])
    h = _ffn_math(h, pre_ref, wg_ref, wu_ref, wd_ref, post_ref)
    up = jnp.dot(p_ref[...].astype(BF16), wup_ref[...], preferred_element_type=F32)
    gate = jnp.dot(h.astype(BF16), wgate_ref[...], preferred_element_type=F32)
    o_ref[...] = h + _rms_norm(up * jax.nn.sigmoid(gate), plepost_ref[...])


def _post(h, attn, ssm_n, p, attn_g, w_a, w_s, mix_post_g, pre_g, wg, wu, wd, post_g, w_up, w_gate,
          ple_post_g, *, tm):
    t, d = h.shape
    da, ds, dp = attn.shape[1], ssm_n.shape[1], p.shape[1]
    ff = wg.shape[1]
    row = lambda w: pl.BlockSpec((tm, w), lambda i: (i, 0))
    vec = _const_spec((1, d))
    return pl.pallas_call(
        _post_body,
        grid=(t // tm,),
        in_specs=[row(d), row(da), row(ds), row(dp), _const_spec((1, da)), _const_spec((da, d)),
                  _const_spec((ds, d)), vec, vec, _const_spec((d, ff)), _const_spec((d, ff)),
                  _const_spec((ff, d)), vec, _const_spec((dp, d)), _const_spec((d, d)), vec],
        out_specs=row(d),
        out_shape=jax.ShapeDtypeStruct((t, d), F32),
        compiler_params=_params("parallel"),
        name="post",
    )(h, attn, ssm_n, p, attn_g, w_a, w_s, mix_post_g, pre_g, wg, wu, wd, post_g, w_up, w_gate, ple_post_g)


def kernel(x, p, positions, ffn1_pre_g, ffn1_w_gate, ffn1_w_up, ffn1_w_down, ffn1_post_g, mix_pre_g, w_in, attn_norm_g, ssm_lam_re, ssm_lam_im, ssm_log_dt, ssm_b_re, ssm_b_im, ssm_c_re, ssm_c_im, ssm_d, ssm_w_glu, ssm_b_glu, ssm_norm_g, w_out, mix_post_g, ffn2_pre_g, ffn2_w_gate, ffn2_w_up, ffn2_w_down, ffn2_post_g, ple_w_up, ple_w_gate, ple_post_g):
    b, s, d = x.shape
    depth = p.shape[0]
    t = b * s
    d_attn = attn_norm_g.shape[1]
    d_ssm = ssm_norm_g.shape[1]
    tm = min(TOKEN_TILE, t)
    ts = min(SSM_TIME_TILE, s)
    vec = lambda a: a.reshape(1, -1)
    bf = lambda a: a.astype(BF16)

    h = x.reshape(t, d)
    pos = positions.reshape(t, 1).astype(F32)
    for i in range(depth):
        h, q, k, v, u = _pre(h, pos, vec(ffn1_pre_g[i]), bf(ffn1_w_gate[i]), bf(ffn1_w_up[i]), bf(ffn1_w_down[i]),
                             vec(ffn1_post_g[i]), vec(mix_pre_g[i]), bf(w_in[i]), d_attn=d_attn, tm=tm)
        attn = _attention(q.reshape(b, s, d_attn), k.reshape(b, s, d_attn), v.reshape(b, s, d_attn))
        ssm_w_in, ssm_w_out, a_re, a_im = _ssm_matrices(
            ssm_lam_re[i], ssm_lam_im[i], ssm_log_dt[i], ssm_b_re[i], ssm_b_im[i], ssm_c_re[i], ssm_c_im[i])
        ssm_n = _ssm(u.reshape(b, s, d_ssm), ssm_w_in, a_re, a_im, ssm_w_out, vec(ssm_d[i]), bf(ssm_w_glu[i]),
                     vec(ssm_b_glu[i]), vec(ssm_norm_g[i]), ts=ts).reshape(t, d_ssm)
        h = _post(h, attn.reshape(t, d_attn), ssm_n, p[i].reshape(t, -1), vec(attn_norm_g[i]),
                  bf(w_out[i][:d_attn]), bf(w_out[i][d_attn:]), vec(mix_post_g[i]),
                  vec(ffn2_pre_g[i]), bf(ffn2_w_gate[i]), bf(ffn2_w_up[i]), bf(ffn2_w_down[i]),
                  vec(ffn2_post_g[i]), bf(ple_w_up[i]), bf(ple_w_gate[i]), vec(ple_post_g[i]), tm=tm)
    return h.reshape(b, s, d)
```

```python
import functools

import numpy as np
import jax
import jax.numpy as jnp
from jax import lax
from jax.experimental import pallas as pl
from jax.experimental.pallas import tpu as pltpu

F32 = jnp.float32
BF16 = jnp.bfloat16

LANES = 128
SUBLANES = 8
MXU_DIM = 256

HEAD_DIM = 64
ROPE_DIM = HEAD_DIM // 4
ROPE_THETA = 500000.0
DILATED_PATTERNS = ((128, 1), (512, 4), (2048, 16))
BAND = 128
SSM_GROUP = 16
SSM_STATE = 64
SSM_CHUNK = 4
SSM_QUAD = 4
NORM_EPS = 1e-6
MASK_VALUE = -1e30

assert all(w // d == BAND for w, d in DILATED_PATTERNS)
DILATIONS = tuple(d for _, d in DILATED_PATTERNS)
ATTN_PLANES = 4
ATTN_UNROLL = 8

TOKEN_TILE = 512
FFN_CHUNKS = 1
SSM_TIME_TILE = 128
VMEM_LIMIT_BYTES = 56 * 1024 * 1024


def _rms_norm(x, g):
    ms = jnp.mean(x * x, axis=-1, keepdims=True)
    return x * lax.rsqrt(ms + NORM_EPS) * g


def _const_spec(shape):
    zeros = (0,) * len(shape)
    return pl.BlockSpec(shape, lambda *_: zeros, pipeline_mode=pl.Buffered(1))


def _layer_spec(layer, shape):
    zeros = (0,) * len(shape)
    return pl.BlockSpec((None,) + tuple(shape), lambda *_: (layer,) + zeros, pipeline_mode=pl.Buffered(1))


def _params(*sem):
    return pltpu.CompilerParams(dimension_semantics=sem, vmem_limit_bytes=VMEM_LIMIT_BYTES)


def _ffn_chunks(ff):
    tiles = -(-ff // MXU_DIM)
    cuts = [min(ff, (tiles * c // FFN_CHUNKS) * MXU_DIM) for c in range(FFN_CHUNKS + 1)]
    return [(lo, hi) for lo, hi in zip(cuts[:-1], cuts[1:]) if hi > lo]


def _ffn_math(x, pre_ref, wg_ref, wu_ref, wd_ref, post_ref):
    xn = _rms_norm(x, pre_ref[...]).astype(BF16)
    acc = None
    for lo, hi in _ffn_chunks(wg_ref.shape[1]):
        g = jnp.dot(xn, wg_ref[:, lo:hi], preferred_element_type=F32)
        u = jnp.dot(xn, wu_ref[:, lo:hi], preferred_element_type=F32)
        mid = (g * jax.nn.sigmoid(g) * u).astype(BF16)
        part = jnp.dot(mid, wd_ref[lo:hi, :], preferred_element_type=F32)
        acc = part if acc is None else acc + part
    return x + 0.5 * _rms_norm(acc, post_ref[...])


def _rope_tables():
    half = ROPE_DIM // 2
    inv_freq = ROPE_THETA ** (-jnp.arange(half, dtype=F32) * (2.0 / ROPE_DIM))
    lane = np.arange(LANES)
    in_head = lane % HEAD_DIM
    freq = jnp.where(in_head < ROPE_DIM, inv_freq[lane % half], 0.0).astype(F32)
    first = (in_head < half).astype(np.float32)
    second = ((in_head >= half) & (in_head < ROPE_DIM)).astype(np.float32)
    return freq.reshape(1, LANES), jnp.asarray(first).reshape(1, LANES), jnp.asarray(second).reshape(1, LANES)


def _pre_body(x_ref, pos_ref, pre_ref, wg_ref, wu_ref, wd_ref, post_ref, g_ref, w_ref,
              freq_ref, first_ref, second_ref, h_ref, q_ref, k_ref, v_ref, u_ref, *, d_attn):
    h = _ffn_math(x_ref[...], pre_ref, wg_ref, wu_ref, wd_ref, post_ref)
    h_ref[...] = h
    xn = _rms_norm(h, g_ref[...]).astype(BF16)
    proj = jnp.dot(xn, w_ref[...], preferred_element_type=F32)
    ang = pos_ref[...] * freq_ref[...]
    cos = jnp.cos(ang)
    sin = jnp.sin(ang)
    half = ROPE_DIM // 2
    sin_first = -sin * first_ref[...]
    sin_second = sin * second_ref[...]

    def rotary(t):
        return (t * cos + pltpu.roll(t, LANES - half, 1) * sin_first
                + pltpu.roll(t, half, 1) * sin_second)

    scale = HEAD_DIM ** -0.5
    for j in range(d_attn // LANES):
        sl = slice(j * LANES, (j + 1) * LANES)
        q_ref[:, sl] = rotary(proj[:, j * LANES:(j + 1) * LANES]) * scale
        k_ref[:, sl] = rotary(proj[:, d_attn + j * LANES:d_attn + (j + 1) * LANES])
    v_ref[...] = proj[:, 2 * d_attn:3 * d_attn]
    u_ref[...] = proj[:, 3 * d_attn:]


def _pre(layer, h, pos, pre_g, wg, wu, wd, post_g, mix_g, w_in, *, d_attn, tm):
    t, d = h.shape
    ff = wg.shape[2]
    n = w_in.shape[2]
    d_ssm = n - 3 * d_attn
    freq, first, second = _rope_tables()
    row = lambda w: pl.BlockSpec((tm, w), lambda i: (i, 0))
    per_layer = functools.partial(_layer_spec, layer)
    lane_tab = _const_spec((1, LANES))
    return pl.pallas_call(
        functools.partial(_pre_body, d_attn=d_attn),
        grid=(t // tm,),
        in_specs=[row(d), row(1), per_layer((1, d)), per_layer((d, ff)), per_layer((d, ff)),
                  per_layer((ff, d)), per_layer((1, d)), per_layer((1, d)), per_layer((d, n)),
                  lane_tab, lane_tab, lane_tab],
        out_specs=[row(d), row(d_attn), row(d_attn), row(d_attn), row(d_ssm)],
        out_shape=[jax.ShapeDtypeStruct((t, d), F32)] + [jax.ShapeDtypeStruct((t, d_attn), F32)] * 3
        + [jax.ShapeDtypeStruct((t, d_ssm), F32)],
        compiler_params=_params("parallel"),
        name="pre",
    )(h, pos, pre_g, wg, wu, wd, post_g, mix_g, w_in, freq, first, second)


def _attn_body(q_ref, k_ref, v_ref, o_ref, planes_s, m_s, l_s, acc_s, *, seq):
    lane = lax.broadcasted_iota(jnp.int32, (BAND, LANES), 1)
    head0 = lane < HEAD_DIM
    qi = lax.broadcasted_iota(jnp.int32, (BAND, BAND), 0)
    kj = lax.broadcasted_iota(jnp.int32, (BAND, BAND), 1)
    cur_ok = kj <= qi
    prev_ok = kj >= qi

    plane_rows = seq // ATTN_PLANES
    ones = jnp.ones((2 * BAND, LANES), BF16)

    for t, ref in enumerate((q_ref, k_ref, v_ref)):
        for r in range(ATTN_PLANES):
            planes_s[t, pl.ds(r * plane_rows, plane_rows), :] = ref[pl.ds(r, plane_rows, stride=ATTN_PLANES), :]

    for pi, d in enumerate(DILATIONS):
        nb = seq // (BAND * d)
        step = max(d // ATTN_PLANES, 1)

        def rows(start):
            return pl.ds(start, BAND) if step == 1 else pl.ds(start, BAND, stride=step)

        def load(t, start):
            return (q_ref, k_ref, v_ref)[t][rows(start), :] if d == 1 else planes_s[t, rows(start), :]

        def block(n, carry):
            r = n // nb
            i = n - r * nb
            base = r if d == 1 else (r % ATTN_PLANES) * plane_rows + r // ATTN_PLANES
            start = base + i * (BAND * step)
            prev = jnp.maximum(start - BAND * step, base)
            q = load(0, start)
            zero = jnp.zeros_like(q)
            q2 = jnp.concatenate([jnp.where(head0, q, zero), jnp.where(head0, zero, q)], axis=0)
            kk = jnp.concatenate([load(1, prev), load(1, start)], axis=0)
            vv = jnp.concatenate([load(2, prev), load(2, start)], axis=0).astype(BF16)
            s = lax.dot_general(q2.astype(BF16), kk.astype(BF16), (((1,), (1,)), ((), ())),
                                preferred_element_type=F32)
            ok = jnp.concatenate([prev_ok & (i > 0), cur_ok], axis=1)
            ok = jnp.concatenate([ok, ok], axis=0)
            s = jnp.where(ok, s, MASK_VALUE)
            m = jnp.max(s, axis=-1, keepdims=True)
            e = jnp.exp(s - m).astype(BF16)
            o2 = jnp.dot(e, jnp.concatenate([vv, ones], axis=1), preferred_element_type=F32)
            m_s[pi, rows(start), :] = jnp.where(head0, m[:BAND], m[BAND:])
            l_s[pi, rows(start), :] = jnp.where(head0, o2[:BAND, LANES:], o2[BAND:, LANES:])
            acc_s[pi, rows(start), :] = jnp.where(head0, o2[:BAND, :LANES], o2[BAND:, :LANES])
            return carry

        lax.fori_loop(0, d * nb, block, 0, unroll=ATTN_UNROLL)

    blocks_per_plane = plane_rows // BAND

    def merge(n, carry):
        r = n // blocks_per_plane
        j = n - r * blocks_per_plane
        orig = pl.ds(r + j * (BAND * ATTN_PLANES), BAND, stride=ATTN_PLANES)
        plane = pl.ds(r * plane_rows + j * BAND, BAND)
        sel = lambda ref: [ref[0, orig, :]] + [ref[pi, plane, :] for pi in range(1, len(DILATIONS))]
        ms, ls, accs = sel(m_s), sel(l_s), sel(acc_s)
        m = functools.reduce(jnp.maximum, ms)
        ws = [jnp.exp(mp - m) for mp in ms]
        num = sum(w * a for w, a in zip(ws, accs))
        den = sum(w * l for w, l in zip(ws, ls))
        o_ref[orig, :] = num / den
        return carry

    lax.fori_loop(0, seq // BAND, merge, 0, unroll=2)


def _attention(q, k, v):
    b, s, d_attn = q.shape
    assert s % (BAND * max(DILATIONS)) == 0 and d_attn % LANES == 0
    assert all(d == 1 or d % ATTN_PLANES == 0 for d in DILATIONS)
    spec = pl.BlockSpec((None, s, LANES), lambda bi, hi: (bi, 0, hi))
    stats = pltpu.VMEM((len(DILATIONS), s, LANES), F32)
    return pl.pallas_call(
        functools.partial(_attn_body, seq=s),
        grid=(b, d_attn // LANES),
        in_specs=[spec, spec, spec],
        out_specs=spec,
        out_shape=jax.ShapeDtypeStruct((b, s, d_attn), F32),
        scratch_shapes=[pltpu.VMEM((3, s, LANES), F32), stats, stats, stats],
        compiler_params=_params("parallel", "parallel"),
        name="attention",
    )(q, k, v)


def _ssm_matrices(lam_re, lam_im, log_dt, b_re, b_im, c_re, c_im):
    g, p = lam_re.shape
    L = SSM_CHUNK
    dt = jnp.exp(log_dt)[:, None]
    mag = jnp.exp(lam_re * dt)
    ar = mag * jnp.cos(lam_im * dt)
    ai = mag * jnp.sin(lam_im * dt)
    wr, wi = ar - 1.0, ai
    den = lam_re * lam_re + lam_im * lam_im
    cr = (wr * lam_re + wi * lam_im) / den
    ci = (wi * lam_re - wr * lam_im) / den
    bbr = cr[..., None] * b_re - ci[..., None] * b_im
    bbi = cr[..., None] * b_im + ci[..., None] * b_re
    pw_r, pw_i = [jnp.ones_like(ar)], [jnp.zeros_like(ai)]
    for _ in range(L):
        pw_r.append(pw_r[-1] * ar - pw_i[-1] * ai)
        pw_i.append(pw_r[-2] * ai + pw_i[-1] * ar)
    er = [c_re * pw_r[n][:, None, :] - c_im * pw_i[n][:, None, :] for n in range(L + 1)]
    ei = [c_re * pw_i[n][:, None, :] + c_im * pw_r[n][:, None, :] for n in range(L + 1)]
    kn = [jnp.einsum("ghp,gpi->ghi", er[n], bbr) - jnp.einsum("ghp,gpi->ghi", ei[n], bbi) for n in range(L)]

    nq = g // SSM_QUAD
    eye = jnp.eye(SSM_QUAD, dtype=F32)

    def diag(m, spec):
        return jnp.einsum(spec, m.reshape((nq, SSM_QUAD) + m.shape[1:]), eye)

    win_parts = []
    for part in range(2):
        per_k = []
        for k in range(L):
            n = L - 1 - k
            if part == 0:
                w = pw_r[n][..., None] * bbr - pw_i[n][..., None] * bbi
            else:
                w = pw_r[n][..., None] * bbi + pw_i[n][..., None] * bbr
            per_k.append(diag(w, "qgph,gk->qghkp"))
        win_parts.append(jnp.stack(per_k, axis=1).reshape(nq, L * SSM_QUAD * SSM_GROUP, SSM_QUAD * p))
    w_in = jnp.concatenate(win_parts, axis=2).astype(BF16)

    zre = jnp.stack([diag(er[r + 1], "qghp,gk->qgpkh") for r in range(L)], axis=3)
    zim = jnp.stack([diag(-ei[r + 1], "qghp,gk->qgpkh") for r in range(L)], axis=3)
    zero = jnp.zeros_like(kn[0])
    xin = jnp.stack([jnp.stack([diag(kn[r - k] if r >= k else zero, "qghi,gk->qgikh") for r in range(L)], axis=3)
                     for k in range(L)], axis=1)
    cols = L * SSM_QUAD * SSM_GROUP
    w_out = jnp.concatenate([zre.reshape(nq, SSM_QUAD * p, cols), zim.reshape(nq, SSM_QUAD * p, cols),
                             xin.reshape(nq, L * SSM_QUAD * SSM_GROUP, cols)], axis=1).astype(BF16)
    al_re = jnp.broadcast_to(pw_r[L].reshape(1, g * p), (SUBLANES, g * p))
    al_im = jnp.broadcast_to(pw_i[L].reshape(1, g * p), (SUBLANES, g * p))
    return w_in, w_out, al_re, al_im


def _ssm_body(u_ref, win_ref, are_ref, aim_ref, wout_ref, dskip_ref, wglu_ref, bglu_ref, g_ref,
              o_ref, z_s, v_s, io_s, *, ts):
    nb = u_ref.shape[0]
    L = SSM_CHUNK
    nc = ts // L
    nq = win_ref.shape[0]
    half = LANES // 2
    sq = win_ref.shape[2] // 2
    n_state = nq * sq
    io_slabs = io_s.shape[0]
    lane = lax.broadcasted_iota(jnp.int32, (nc * nb, LANES), 1)
    low = lane < half

    @pl.when(pl.program_id(0) == 0)
    def _():
        z_s[...] = jnp.zeros(z_s.shape, F32)

    def batch_rows(bi):
        return pl.ds(bi, ts, stride=nb)

    for bi in range(nb):
        for j in range(io_slabs):
            io_s[j, batch_rows(bi), :] = u_ref[bi, :, j * LANES:(j + 1) * LANES]
    def step_rows(j, k):
        return jnp.concatenate([io_s[j, pl.ds((m * L + k) * nb, nb), :] for m in range(nc)], axis=0)

    us = [[step_rows(j, k) for k in range(L)] for j in range(io_slabs)]
    xq = []
    for j in range(io_slabs):
        rolled = [pltpu.roll(us[j][k], half, 1) for k in range(L)]
        for hi in range(2):
            pieces = []
            for k in range(0, L, 2):
                if hi == 0:
                    pieces.append(jnp.where(low, us[j][k], rolled[k + 1]))
                else:
                    pieces.append(jnp.where(low, rolled[k], us[j][k + 1]))
            xq.append(jnp.concatenate(pieces, axis=1).astype(BF16))

    for q in range(nq):
        v = jnp.dot(xq[q], win_ref[q], preferred_element_type=F32)
        v_s[:, q * sq:(q + 1) * sq] = v[:, :sq]
        v_s[:, n_state + q * sq:n_state + (q + 1) * sq] = v[:, sq:]

    for j in range(n_state // LANES):
        re = slice(j * LANES, (j + 1) * LANES)
        im = slice(n_state + j * LANES, n_state + (j + 1) * LANES)
        ar = are_ref[:, re]
        ai = aim_ref[:, re]
        zr = z_s[:, re]
        zi = z_s[:, im]
        for m in range(nc):
            row = slice(m * nb, (m + 1) * nb)
            nzr = ar * zr - ai * zi + v_s[row, re]
            nzi = ar * zi + ai * zr + v_s[row, im]
            v_s[row, re] = zr
            v_s[row, im] = zi
            zr, zi = nzr, nzi
        z_s[:, re] = zr
        z_s[:, im] = zi

    yq = []
    for q in range(nq):
        lhs = jnp.concatenate([v_s[:, q * sq:(q + 1) * sq].astype(BF16),
                               v_s[:, n_state + q * sq:n_state + (q + 1) * sq].astype(BF16), xq[q]], axis=1)
        yq.append(jnp.dot(lhs, wout_ref[q], preferred_element_type=F32))
    y_steps, u_steps = [], []
    for r in range(L):
        slabs = []
        for j in range(io_slabs):
            lo_src = yq[2 * j][:, (r // 2) * LANES:(r // 2 + 1) * LANES]
            hi_src = yq[2 * j + 1][:, (r // 2) * LANES:(r // 2 + 1) * LANES]
            if r % 2 == 0:
                slabs.append(jnp.where(low, lo_src, pltpu.roll(hi_src, half, 1)))
            else:
                slabs.append(jnp.where(low, pltpu.roll(lo_src, half, 1), hi_src))
        y_steps.append(jnp.concatenate(slabs, axis=1))
        u_steps.append(jnp.concatenate([us[j][r] for j in range(io_slabs)], axis=1))
    y = jnp.concatenate(y_steps, axis=0)
    u = jnp.concatenate(u_steps, axis=0)
    y = jax.nn.gelu(y + dskip_ref[...] * u)
    gate = jnp.dot(y.astype(BF16), wglu_ref[...], preferred_element_type=F32) + bglu_ref[...]
    y = _rms_norm(y * jax.nn.sigmoid(gate), g_ref[...])
    for r in range(L):
        for m in range(nc):
            src = (r * nc + m) * nb
            for j in range(io_slabs):
                io_s[j, pl.ds((m * L + r) * nb, nb), :] = y[src:src + nb, j * LANES:(j + 1) * LANES]
    for bi in range(nb):
        o_ref[bi] = jnp.concatenate([io_s[j, batch_rows(bi), :] for j in range(io_slabs)],
                                    axis=1).astype(o_ref.dtype)


def _ssm(layer, u, w_in, a_re, a_im, w_out, d_skip, w_glu, b_glu, norm_g, *, ts):
    b, s, d_ssm = u.shape
    assert b == SUBLANES, "the S5 scan keeps the batch on the sublane axis"
    assert SSM_CHUNK * SSM_QUAD * SSM_GROUP == MXU_DIM and SSM_QUAD * SSM_GROUP * 2 == LANES and SSM_CHUNK % 2 == 0
    n_state = a_re.shape[2]
    tile = pl.BlockSpec((b, ts, d_ssm), lambda i: (0, i, 0))
    per_layer = functools.partial(_layer_spec, layer)
    return pl.pallas_call(
        functools.partial(_ssm_body, ts=ts),
        grid=(s // ts,),
        in_specs=[tile, per_layer(w_in.shape[1:]), per_layer(a_re.shape[1:]), per_layer(a_im.shape[1:]),
                  per_layer(w_out.shape[1:]), per_layer((1, d_ssm)), per_layer((d_ssm, d_ssm)),
                  per_layer((1, d_ssm)), per_layer((1, d_ssm))],
        out_specs=tile,
        out_shape=jax.ShapeDtypeStruct((b, s, d_ssm), BF16),
        scratch_shapes=[pltpu.VMEM((b, 2 * n_state), F32),
                        pltpu.VMEM((ts // SSM_CHUNK * b, 2 * n_state), F32),
                        pltpu.VMEM((d_ssm // LANES, ts * b, LANES), F32)],
        compiler_params=_params("arbitrary"),
        name="ssm",
    )(u, w_in, a_re, a_im, w_out, d_skip, w_glu, b_glu, norm_g)


def _post_body(h_ref, attn_ref, ssm_ref, p_ref, ga_ref, wa_ref, ws_ref, mixpost_ref,
               pre_ref, wg_ref, wu_ref, wd_ref, post_ref, wup_ref, wgate_ref, plepost_ref, o_ref):
    an = _rms_norm(attn_ref[...], ga_ref[...]).astype(BF16)
    mix = jnp.dot(an, wa_ref[...], preferred_element_type=F32)
    mix += jnp.dot(ssm_ref[...], ws_ref[...], preferred_element_type=F32)
    h = h_ref[...] + _rms_norm(mix, mixpost_ref[...])
    h = _ffn_math(h, pre_ref, wg_ref, wu_ref, wd_ref, post_ref)
    up = jnp.dot(p_ref[...].astype(BF16), wup_ref[...], preferred_element_type=F32)
    gate = jnp.dot(h.astype(BF16), wgate_ref[...], preferred_element_type=F32)
    o_ref[...] = h + _rms_norm(up * jax.nn.sigmoid(gate), plepost_ref[...])


def _post(layer, h, attn, ssm_n, p, attn_g, w_out, mix_post_g, pre_g, wg, wu, wd, post_g, w_up, w_gate,
          ple_post_g, *, tm):
    t, d = h.shape
    da, ds, dp = attn.shape[1], ssm_n.shape[1], p.shape[2]
    ff = wg.shape[2]
    row = lambda w: pl.BlockSpec((tm, w), lambda i: (i, 0))
    per_layer = functools.partial(_layer_spec, layer)
    vec = per_layer((1, d))
    w_attn = pl.BlockSpec((None, da, d), lambda i: (layer, 0, 0), pipeline_mode=pl.Buffered(1))
    w_ssm = pl.BlockSpec((None, ds, d), lambda i: (layer, da // ds, 0), pipeline_mode=pl.Buffered(1))
    return pl.pallas_call(
        _post_body,
        grid=(t // tm,),
        in_specs=[row(d), row(da), row(ds), pl.BlockSpec((None, tm, dp), lambda i: (layer, i, 0)),
                  per_layer((1, da)), w_attn, w_ssm, vec, vec, per_layer((d, ff)), per_layer((d, ff)),
                  per_layer((ff, d)), vec, per_layer((dp, d)), per_layer((d, d)), vec],
        out_specs=row(d),
        out_shape=jax.ShapeDtypeStruct((t, d), F32),
        compiler_params=_params("parallel"),
        name="post",
    )(h, attn, ssm_n, p, attn_g, w_out, w_out, mix_post_g, pre_g, wg, wu, wd, post_g, w_up, w_gate, ple_post_g)


def kernel(x, p, positions, ffn1_pre_g, ffn1_w_gate, ffn1_w_up, ffn1_w_down, ffn1_post_g, mix_pre_g, w_in, attn_norm_g, ssm_lam_re, ssm_lam_im, ssm_log_dt, ssm_b_re, ssm_b_im, ssm_c_re, ssm_c_im, ssm_d, ssm_w_glu, ssm_b_glu, ssm_norm_g, w_out, mix_post_g, ffn2_pre_g, ffn2_w_gate, ffn2_w_up, ffn2_w_down, ffn2_post_g, ple_w_up, ple_w_gate, ple_post_g):
    b, s, d = x.shape
    depth = p.shape[0]
    t = b * s
    d_attn = attn_norm_g.shape[1]
    d_ssm = ssm_norm_g.shape[1]
    assert d_attn % d_ssm == 0
    tm = min(TOKEN_TILE, t)
    ts = min(SSM_TIME_TILE, s)
    vec = lambda a: a.reshape(depth, 1, -1)
    bf = lambda a: a.astype(BF16)

    ffn1 = (vec(ffn1_pre_g), bf(ffn1_w_gate), bf(ffn1_w_up), bf(ffn1_w_down), vec(ffn1_post_g))
    ffn2 = (vec(ffn2_pre_g), bf(ffn2_w_gate), bf(ffn2_w_up), bf(ffn2_w_down), vec(ffn2_post_g))
    mix_g, w_in_bf, w_out_bf = vec(mix_pre_g), bf(w_in), bf(w_out)
    ssm_w_in, ssm_w_out, a_re, a_im = jax.vmap(_ssm_matrices)(
        ssm_lam_re, ssm_lam_im, ssm_log_dt, ssm_b_re, ssm_b_im, ssm_c_re, ssm_c_im)
    ssm_rest = (vec(ssm_d), bf(ssm_w_glu), vec(ssm_b_glu), vec(ssm_norm_g))
    ple = (bf(ple_w_up), bf(ple_w_gate), vec(ple_post_g))
    p_rows = p.reshape(depth, t, -1)

    h = x.reshape(t, d)
    pos = positions.reshape(t, 1).astype(F32)
    for i in range(depth):
        h, q, k, v, u = _pre(i, h, pos, *ffn1, mix_g, w_in_bf, d_attn=d_attn, tm=tm)
        attn = _attention(q.reshape(b, s, d_attn), k.reshape(b, s, d_attn), v.reshape(b, s, d_attn))
        ssm_n = _ssm(i, u.reshape(b, s, d_ssm), ssm_w_in, a_re, a_im, ssm_w_out, *ssm_rest, ts=ts)
        h = _post(i, h, attn.reshape(t, d_attn), ssm_n.reshape(t, d_ssm), p_rows, vec(attn_norm_g), w_out_bf,
                  vec(mix_post_g), *ffn2, *ple, tm=tm)
    return h.reshape(b, s, d)
```

```python
import functools

import numpy as np
import jax
import jax.numpy as jnp
from jax import lax
from jax.experimental import pallas as pl
from jax.experimental.pallas import tpu as pltpu

F32 = jnp.float32
BF16 = jnp.bfloat16

LANES = 128
SUBLANES = 8
MXU_DIM = 256

HEAD_DIM = 64
ROPE_DIM = HEAD_DIM // 4
ROPE_THETA = 500000.0
DILATED_PATTERNS = ((128, 1), (512, 4), (2048, 16))
BAND = 128
SSM_GROUP = 16
SSM_STATE = 64
SSM_CHUNK = 4
SSM_QUAD = 4
NORM_EPS = 1e-6
MASK_VALUE = -1e30

assert all(w // d == BAND for w, d in DILATED_PATTERNS)
DILATIONS = tuple(d for _, d in DILATED_PATTERNS)
ATTN_PLANES = 4
ATTN_UNROLL = 16

TOKEN_TILE = 512
ROPE_TILE = 2048
FFN_CHUNKS = 1
SSM_TIME_TILE = 128
VMEM_LIMIT_BYTES = 56 * 1024 * 1024


def _rms_norm(x, g):
    ms = jnp.mean(x * x, axis=-1, keepdims=True)
    return x * lax.rsqrt(ms + NORM_EPS) * g


def _const_spec(shape):
    zeros = (0,) * len(shape)
    return pl.BlockSpec(shape, lambda *_: zeros, pipeline_mode=pl.Buffered(1))


def _layer_spec(layer, shape):
    zeros = (0,) * len(shape)
    return pl.BlockSpec((None,) + tuple(shape), lambda *_: (layer,) + zeros, pipeline_mode=pl.Buffered(1))


def _params(*sem):
    return pltpu.CompilerParams(dimension_semantics=sem, vmem_limit_bytes=VMEM_LIMIT_BYTES)


def _ffn_chunks(ff):
    tiles = -(-ff // MXU_DIM)
    cuts = [min(ff, (tiles * c // FFN_CHUNKS) * MXU_DIM) for c in range(FFN_CHUNKS + 1)]
    return [(lo, hi) for lo, hi in zip(cuts[:-1], cuts[1:]) if hi > lo]


def _ffn_math(x, pre_ref, wg_ref, wu_ref, wd_ref, post_ref):
    xn = _rms_norm(x, pre_ref[...]).astype(BF16)
    acc = None
    for lo, hi in _ffn_chunks(wg_ref.shape[1]):
        g = jnp.dot(xn, wg_ref[:, lo:hi], preferred_element_type=F32)
        u = jnp.dot(xn, wu_ref[:, lo:hi], preferred_element_type=F32)
        mid = (g * jax.nn.sigmoid(g) * u).astype(BF16)
        part = jnp.dot(mid, wd_ref[lo:hi, :], preferred_element_type=F32)
        acc = part if acc is None else acc + part
    return x + 0.5 * _rms_norm(acc, post_ref[...])


def _rope_lane_tables():
    half = ROPE_DIM // 2
    inv_freq = ROPE_THETA ** (-jnp.arange(half, dtype=F32) * (2.0 / ROPE_DIM))
    lane = np.arange(LANES)
    in_head = lane % HEAD_DIM
    freq = jnp.where(in_head < ROPE_DIM, inv_freq[lane % half], 0.0).astype(F32)
    first = (in_head < half).astype(np.float32)
    second = ((in_head >= half) & (in_head < ROPE_DIM)).astype(np.float32)
    return freq.reshape(1, LANES), jnp.asarray(first).reshape(1, LANES), jnp.asarray(second).reshape(1, LANES)


def _rope_body(pos_ref, freq_ref, first_ref, second_ref, cos_ref, sin_first_ref, sin_second_ref):
    ang = pos_ref[...] * freq_ref[...]
    sin = jnp.sin(ang)
    cos_ref[...] = jnp.cos(ang)
    sin_first_ref[...] = -sin * first_ref[...]
    sin_second_ref[...] = sin * second_ref[...]


def _rope(pos, *, tm):
    t = pos.shape[0]
    freq, first, second = _rope_lane_tables()
    row = lambda w: pl.BlockSpec((tm, w), lambda i: (i, 0))
    lane_tab = _const_spec((1, LANES))
    return pl.pallas_call(
        _rope_body,
        grid=(t // tm,),
        in_specs=[row(1), lane_tab, lane_tab, lane_tab],
        out_specs=[row(LANES)] * 3,
        out_shape=[jax.ShapeDtypeStruct((t, LANES), F32)] * 3,
        compiler_params=_params("parallel"),
        name="rope",
    )(pos, freq, first, second)


def _pre_body(x_ref, cos_ref, sin_first_ref, sin_second_ref, pre_ref, wg_ref, wu_ref, wd_ref, post_ref,
              g_ref, w_ref, h_ref, q_ref, k_ref, v_ref, u_ref, *, d_attn):
    h = _ffn_math(x_ref[...], pre_ref, wg_ref, wu_ref, wd_ref, post_ref)
    h_ref[...] = h
    xn = _rms_norm(h, g_ref[...]).astype(BF16)
    proj = jnp.dot(xn, w_ref[...], preferred_element_type=F32)
    half = ROPE_DIM // 2

    def rotary(t):
        return (t * cos_ref[...] + pltpu.roll(t, LANES - half, 1) * sin_first_ref[...]
                + pltpu.roll(t, half, 1) * sin_second_ref[...])

    scale = HEAD_DIM ** -0.5
    for j in range(d_attn // LANES):
        sl = slice(j * LANES, (j + 1) * LANES)
        q_ref[:, sl] = rotary(proj[:, j * LANES:(j + 1) * LANES]) * scale
        k_ref[:, sl] = rotary(proj[:, d_attn + j * LANES:d_attn + (j + 1) * LANES])
    v_ref[...] = proj[:, 2 * d_attn:3 * d_attn]
    u_ref[...] = proj[:, 3 * d_attn:]


def _pre(layer, h, rope, pre_g, wg, wu, wd, post_g, mix_g, w_in, *, d_attn, tm):
    t, d = h.shape
    ff = wg.shape[2]
    n = w_in.shape[2]
    d_ssm = n - 3 * d_attn
    row = lambda w: pl.BlockSpec((tm, w), lambda i: (i, 0))
    per_layer = functools.partial(_layer_spec, layer)
    return pl.pallas_call(
        functools.partial(_pre_body, d_attn=d_attn),
        grid=(t // tm,),
        in_specs=[row(d), row(LANES), row(LANES), row(LANES), per_layer((1, d)), per_layer((d, ff)),
                  per_layer((d, ff)), per_layer((ff, d)), per_layer((1, d)), per_layer((1, d)), per_layer((d, n))],
        out_specs=[row(d), row(d_attn), row(d_attn), row(d_attn), row(d_ssm)],
        out_shape=[jax.ShapeDtypeStruct((t, d), F32)] + [jax.ShapeDtypeStruct((t, d_attn), F32)] * 3
        + [jax.ShapeDtypeStruct((t, d_ssm), F32)],
        compiler_params=_params("parallel"),
        name="pre",
    )(h, *rope, pre_g, wg, wu, wd, post_g, mix_g, w_in)


def _attn_body(q_ref, k_ref, v_ref, o_ref, planes_s, m_s, l_s, acc_s, *, seq):
    lane = lax.broadcasted_iota(jnp.int32, (BAND, LANES), 1)
    head0 = lane < HEAD_DIM
    qi = lax.broadcasted_iota(jnp.int32, (BAND, BAND), 0)
    kj = lax.broadcasted_iota(jnp.int32, (BAND, BAND), 1)
    cur_ok = kj <= qi
    prev_ok = kj >= qi

    plane_rows = seq // ATTN_PLANES
    ones = jnp.ones((2 * BAND, LANES), BF16)

    for t, ref in enumerate((q_ref, k_ref, v_ref)):
        for r in range(ATTN_PLANES):
            planes_s[t, pl.ds(r * plane_rows, plane_rows), :] = ref[pl.ds(r, plane_rows, stride=ATTN_PLANES), :]

    for pi, d in enumerate(DILATIONS):
        nb = seq // (BAND * d)
        step = max(d // ATTN_PLANES, 1)

        def rows(start):
            return pl.ds(start, BAND) if step == 1 else pl.ds(start, BAND, stride=step)

        def load(t, start):
            return (q_ref, k_ref, v_ref)[t][rows(start), :] if d == 1 else planes_s[t, rows(start), :]

        def block(n, carry):
            r = n // nb
            i = n - r * nb
            base = r if d == 1 else (r % ATTN_PLANES) * plane_rows + r // ATTN_PLANES
            start = base + i * (BAND * step)
            prev = jnp.maximum(start - BAND * step, base)
            q = load(0, start)
            zero = jnp.zeros_like(q)
            q2 = jnp.concatenate([jnp.where(head0, q, zero), jnp.where(head0, zero, q)], axis=0)
            kk = jnp.concatenate([load(1, prev), load(1, start)], axis=0)
            vv = jnp.concatenate([load(2, prev), load(2, start)], axis=0).astype(BF16)
            s = lax.dot_general(q2.astype(BF16), kk.astype(BF16), (((1,), (1,)), ((), ())),
                                preferred_element_type=F32)
            ok = jnp.concatenate([prev_ok & (i > 0), cur_ok], axis=1)
            ok = jnp.concatenate([ok, ok], axis=0)
            s = jnp.where(ok, s, MASK_VALUE)
            m = jnp.max(s, axis=-1, keepdims=True)
            e = jnp.exp(s - m).astype(BF16)
            o2 = jnp.dot(e, jnp.concatenate([vv, ones], axis=1), preferred_element_type=F32)
            m_s[pi, rows(start), :] = jnp.where(head0, m[:BAND], m[BAND:])
            l_s[pi, rows(start), :] = jnp.where(head0, o2[:BAND, LANES:], o2[BAND:, LANES:])
            acc_s[pi, rows(start), :] = jnp.where(head0, o2[:BAND, :LANES], o2[BAND:, :LANES])
            return carry

        lax.fori_loop(0, d * nb, block, 0, unroll=ATTN_UNROLL)

    blocks_per_plane = plane_rows // BAND

    def merge(n, carry):
        r = n // blocks_per_plane
        j = n - r * blocks_per_plane
        orig = pl.ds(r + j * (BAND * ATTN_PLANES), BAND, stride=ATTN_PLANES)
        plane = pl.ds(r * plane_rows + j * BAND, BAND)
        sel = lambda ref: [ref[0, orig, :]] + [ref[pi, plane, :] for pi in range(1, len(DILATIONS))]
        ms, ls, accs = sel(m_s), sel(l_s), sel(acc_s)
        m = functools.reduce(jnp.maximum, ms)
        ws = [jnp.exp(mp - m) for mp in ms]
        num = sum(w * a for w, a in zip(ws, accs))
        den = sum(w * l for w, l in zip(ws, ls))
        o_ref[orig, :] = num / den
        return carry

    lax.fori_loop(0, seq // BAND, merge, 0, unroll=2)


def _attention(q, k, v):
    b, s, d_attn = q.shape
    assert s % (BAND * max(DILATIONS)) == 0 and d_attn % LANES == 0
    assert all(d == 1 or d % ATTN_PLANES == 0 for d in DILATIONS)
    spec = pl.BlockSpec((None, s, LANES), lambda bi, hi: (bi, 0, hi))
    stats = pltpu.VMEM((len(DILATIONS), s, LANES), F32)
    return pl.pallas_call(
        functools.partial(_attn_body, seq=s),
        grid=(b, d_attn // LANES),
        in_specs=[spec, spec, spec],
        out_specs=spec,
        out_shape=jax.ShapeDtypeStruct((b, s, d_attn), F32),
        scratch_shapes=[pltpu.VMEM((3, s, LANES), F32), stats, stats, stats],
        compiler_params=_params("parallel", "parallel"),
        name="attention",
    )(q, k, v)


def _ssm_matrices(lam_re, lam_im, log_dt, b_re, b_im, c_re, c_im):
    g, p = lam_re.shape
    L, Q, H = SSM_CHUNK, SSM_QUAD, SSM_GROUP
    nq = g // Q
    dt = jnp.exp(log_dt)[:, None]
    mag = jnp.exp(lam_re * dt)
    ar = mag * jnp.cos(lam_im * dt)
    ai = mag * jnp.sin(lam_im * dt)
    wr, wi = ar - 1.0, ai
    den = lam_re * lam_re + lam_im * lam_im
    cr = (wr * lam_re + wi * lam_im) / den
    ci = (wi * lam_re - wr * lam_im) / den
    bbr = cr[..., None] * b_re - ci[..., None] * b_im
    bbi = cr[..., None] * b_im + ci[..., None] * b_re
    pw_r, pw_i = [jnp.ones_like(ar)], [jnp.zeros_like(ai)]
    for _ in range(L):
        pw_r.append(pw_r[-1] * ar - pw_i[-1] * ai)
        pw_i.append(pw_r[-2] * ai + pw_i[-1] * ar)
    pr, pi = jnp.stack(pw_r), jnp.stack(pw_i)

    own = np.eye(Q, dtype=np.float32)
    mask_hp = jnp.asarray(np.kron(own, np.ones((H, p), np.float32)))
    mask_ph = jnp.asarray(np.kron(own, np.ones((p, H), np.float32)))
    mask_hh = jnp.asarray(np.kron(own, np.ones((H, H), np.float32)))

    def spread(m, mask):
        lead = m.shape[:-3]
        a, b = m.shape[-2:]
        rows = m.reshape(lead + (nq, Q * a, b))
        return jnp.tile(rows, (1,) * (len(lead) + 2) + (Q,)) * mask

    bt_r, bt_i = jnp.swapaxes(bbr, 1, 2), jnp.swapaxes(bbi, 1, 2)
    nr, ni = pr[L - 1::-1][:, :, None, :], pi[L - 1::-1][:, :, None, :]
    win_r = spread(nr * bt_r - ni * bt_i, mask_hp)
    win_i = spread(nr * bt_i + ni * bt_r, mask_hp)
    w_in = jnp.concatenate([win_r, win_i], axis=-1)
    w_in = jnp.swapaxes(w_in, 0, 1).reshape(nq, L * Q * H, 2 * Q * p).astype(BF16)

    ct_r, ct_i = jnp.swapaxes(c_re, 1, 2), jnp.swapaxes(c_im, 1, 2)
    er = ct_r * pr[1:, :, :, None] - ct_i * pi[1:, :, :, None]
    ei = ct_r * pi[1:, :, :, None] + ct_i * pr[1:, :, :, None]
    zre = spread(er, mask_ph)
    zim = spread(-ei, mask_ph)
    e0r = ct_r * pr[:L, :, :, None] - ct_i * pi[:L, :, :, None]
    e0i = ct_r * pi[:L, :, :, None] + ct_i * pr[:L, :, :, None]
    knt = jnp.einsum("gpi,ngph->ngih", bbr, e0r) - jnp.einsum("gpi,ngph->ngih", bbi, e0i)
    kq = spread(knt, mask_hh)
    zero = jnp.zeros_like(kq[0])
    xin = jnp.stack([jnp.concatenate([kq[r - k] if r >= k else zero for r in range(L)], axis=-1)
                     for k in range(L)], axis=1)
    cols = lambda z: jnp.concatenate([z[r] for r in range(L)], axis=-1)
    w_out = jnp.concatenate([cols(zre), cols(zim), xin.reshape(nq, L * Q * H, L * Q * H)], axis=1).astype(BF16)
    al_re = jnp.broadcast_to(pw_r[L].reshape(1, g * p), (SUBLANES, g * p))
    al_im = jnp.broadcast_to(pw_i[L].reshape(1, g * p), (SUBLANES, g * p))
    return w_in, w_out, al_re, al_im


def _ssm_body(u_ref, win_ref, are_ref, aim_ref, wout_ref, dskip_ref, wglu_ref, bglu_ref, g_ref,
              o_ref, z_s, v_s, io_s, *, ts):
    nb = u_ref.shape[0]
    L = SSM_CHUNK
    nc = ts // L
    nq = win_ref.shape[0]
    half = LANES // 2
    sq = win_ref.shape[2] // 2
    n_state = nq * sq
    io_slabs = io_s.shape[0]
    lane = lax.broadcasted_iota(jnp.int32, (nc * nb, LANES), 1)
    low = lane < half

    @pl.when(pl.program_id(0) == 0)
    def _():
        z_s[...] = jnp.zeros(z_s.shape, F32)

    def batch_rows(bi):
        return pl.ds(bi, ts, stride=nb)

    for bi in range(nb):
        for j in range(io_slabs):
            io_s[j, batch_rows(bi), :] = u_ref[bi, :, j * LANES:(j + 1) * LANES]
    def step_rows(j, k):
        return jnp.concatenate([io_s[j, pl.ds((m * L + k) * nb, nb), :] for m in range(nc)], axis=0)

    us = [[step_rows(j, k) for k in range(L)] for j in range(io_slabs)]
    xq = []
    for j in range(io_slabs):
        rolled = [pltpu.roll(us[j][k], half, 1) for k in range(L)]
        for hi in range(2):
            pieces = []
            for k in range(0, L, 2):
                if hi == 0:
                    pieces.append(jnp.where(low, us[j][k], rolled[k + 1]))
                else:
                    pieces.append(jnp.where(low, rolled[k], us[j][k + 1]))
            xq.append(jnp.concatenate(pieces, axis=1).astype(BF16))

    for q in range(nq):
        v = jnp.dot(xq[q], win_ref[q], preferred_element_type=F32)
        v_s[:, q * sq:(q + 1) * sq] = v[:, :sq]
        v_s[:, n_state + q * sq:n_state + (q + 1) * sq] = v[:, sq:]

    for j in range(n_state // LANES):
        re = slice(j * LANES, (j + 1) * LANES)
        im = slice(n_state + j * LANES, n_state + (j + 1) * LANES)
        ar = are_ref[:, re]
        ai = aim_ref[:, re]
        zr = z_s[:, re]
        zi = z_s[:, im]
        for m in range(nc):
            row = slice(m * nb, (m + 1) * nb)
            nzr = ar * zr - ai * zi + v_s[row, re]
            nzi = ar * zi + ai * zr + v_s[row, im]
            v_s[row, re] = zr
            v_s[row, im] = zi
            zr, zi = nzr, nzi
        z_s[:, re] = zr
        z_s[:, im] = zi

    yq = []
    for q in range(nq):
        lhs = jnp.concatenate([v_s[:, q * sq:(q + 1) * sq].astype(BF16),
                               v_s[:, n_state + q * sq:n_state + (q + 1) * sq].astype(BF16), xq[q]], axis=1)
        yq.append(jnp.dot(lhs, wout_ref[q], preferred_element_type=F32))
    y_steps, u_steps = [], []
    for r in range(L):
        slabs = []
        for j in range(io_slabs):
            lo_src = yq[2 * j][:, (r // 2) * LANES:(r // 2 + 1) * LANES]
            hi_src = yq[2 * j + 1][:, (r // 2) * LANES:(r // 2 + 1) * LANES]
            if r % 2 == 0:
                slabs.append(jnp.where(low, lo_src, pltpu.roll(hi_src, half, 1)))
            else:
                slabs.append(jnp.where(low, pltpu.roll(lo_src, half, 1), hi_src))
        y_steps.append(jnp.concatenate(slabs, axis=1))
        u_steps.append(jnp.concatenate([us[j][r] for j in range(io_slabs)], axis=1))
    y = jnp.concatenate(y_steps, axis=0)
    u = jnp.concatenate(u_steps, axis=0)
    y = jax.nn.gelu(y + dskip_ref[...] * u)
    gate = jnp.dot(y.astype(BF16), wglu_ref[...], preferred_element_type=F32) + bglu_ref[...]
    y = _rms_norm(y * jax.nn.sigmoid(gate), g_ref[...])
    for r in range(L):
        for m in range(nc):
            src = (r * nc + m) * nb
            for j in range(io_slabs):
                io_s[j, pl.ds((m * L + r) * nb, nb), :] = y[src:src + nb, j * LANES:(j + 1) * LANES]
    for bi in range(nb):
        o_ref[bi] = jnp.concatenate([io_s[j, batch_rows(bi), :] for j in range(io_slabs)],
                                    axis=1).astype(o_ref.dtype)


def _ssm(layer, u, w_in, a_re, a_im, w_out, d_skip, w_glu, b_glu, norm_g, *, ts):
    b, s, d_ssm = u.shape
    assert b == SUBLANES, "the S5 scan keeps the batch on the sublane axis"
    assert SSM_CHUNK * SSM_QUAD * SSM_GROUP == MXU_DIM and SSM_QUAD * SSM_GROUP * 2 == LANES and SSM_CHUNK % 2 == 0
    n_state = a_re.shape[2]
    tile = pl.BlockSpec((b, ts, d_ssm), lambda i: (0, i, 0))
    per_layer = functools.partial(_layer_spec, layer)
    return pl.pallas_call(
        functools.partial(_ssm_body, ts=ts),
        grid=(s // ts,),
        in_specs=[tile, per_layer(w_in.shape[1:]), per_layer(a_re.shape[1:]), per_layer(a_im.shape[1:]),
                  per_layer(w_out.shape[1:]), per_layer((1, d_ssm)), per_layer((d_ssm, d_ssm)),
                  per_layer((1, d_ssm)), per_layer((1, d_ssm))],
        out_specs=tile,
        out_shape=jax.ShapeDtypeStruct((b, s, d_ssm), BF16),
        scratch_shapes=[pltpu.VMEM((b, 2 * n_state), F32),
                        pltpu.VMEM((ts // SSM_CHUNK * b, 2 * n_state), F32),
                        pltpu.VMEM((d_ssm // LANES, ts * b, LANES), F32)],
        compiler_params=_params("arbitrary"),
        name="ssm",
    )(u, w_in, a_re, a_im, w_out, d_skip, w_glu, b_glu, norm_g)


def _post_body(h_ref, attn_ref, ssm_ref, p_ref, ga_ref, wa_ref, ws_ref, mixpost_ref,
               pre_ref, wg_ref, wu_ref, wd_ref, post_ref, wup_ref, wgate_ref, plepost_ref, o_ref):
    an = _rms_norm(attn_ref[...], ga_ref[...]).astype(BF16)
    mix = jnp.dot(an, wa_ref[...], preferred_element_type=F32)
    mix += jnp.dot(ssm_ref[...], ws_ref[...], preferred_element_type=F32)
    h = h_ref[...] + _rms_norm(mix, mixpost_ref[...])
    h = _ffn_math(h, pre_ref, wg_ref, wu_ref, wd_ref, post_ref)
    up = jnp.dot(p_ref[...].astype(BF16), wup_ref[...], preferred_element_type=F32)
    gate = jnp.dot(h.astype(BF16), wgate_ref[...], preferred_element_type=F32)
    o_ref[...] = h + _rms_norm(up * jax.nn.sigmoid(gate), plepost_ref[...])


def _post(layer, h, attn, ssm_n, p, attn_g, w_out, mix_post_g, pre_g, wg, wu, wd, post_g, w_up, w_gate,
          ple_post_g, *, tm):
    t, d = h.shape
    da, ds, dp = attn.shape[1], ssm_n.shape[1], p.shape[2]
    ff = wg.shape[2]
    row = lambda w: pl.BlockSpec((tm, w), lambda i: (i, 0))
    per_layer = functools.partial(_layer_spec, layer)
    vec = per_layer((1, d))
    w_attn = pl.BlockSpec((None, da, d), lambda i: (layer, 0, 0), pipeline_mode=pl.Buffered(1))
    w_ssm = pl.BlockSpec((None, ds, d), lambda i: (layer, da // ds, 0), pipeline_mode=pl.Buffered(1))
    return pl.pallas_call(
        _post_body,
        grid=(t // tm,),
        in_specs=[row(d), row(da), row(ds), pl.BlockSpec((None, tm, dp), lambda i: (layer, i, 0)),
                  per_layer((1, da)), w_attn, w_ssm, vec, vec, per_layer((d, ff)), per_layer((d, ff)),
                  per_layer((ff, d)), vec, per_layer((dp, d)), per_layer((d, d)), vec],
        out_specs=row(d),
        out_shape=jax.ShapeDtypeStruct((t, d), F32),
        compiler_params=_params("parallel"),
        name="post",
    )(h, attn, ssm_n, p, attn_g, w_out, w_out, mix_post_g, pre_g, wg, wu, wd, post_g, w_up, w_gate, ple_post_g)


def kernel(x, p, positions, ffn1_pre_g, ffn1_w_gate, ffn1_w_up, ffn1_w_down, ffn1_post_g, mix_pre_g, w_in, attn_norm_g, ssm_lam_re, ssm_lam_im, ssm_log_dt, ssm_b_re, ssm_b_im, ssm_c_re, ssm_c_im, ssm_d, ssm_w_glu, ssm_b_glu, ssm_norm_g, w_out, mix_post_g, ffn2_pre_g, ffn2_w_gate, ffn2_w_up, ffn2_w_down, ffn2_post_g, ple_w_up, ple_w_gate, ple_post_g):
    b, s, d = x.shape
    depth = p.shape[0]
    t = b * s
    d_attn = attn_norm_g.shape[1]
    d_ssm = ssm_norm_g.shape[1]
    assert d_attn % d_ssm == 0
    tm = min(TOKEN_TILE, t)
    ts = min(SSM_TIME_TILE, s)
    vec = lambda a: a.reshape(depth, 1, -1)
    bf = lambda a: a.astype(BF16)

    ffn1 = (vec(ffn1_pre_g), bf(ffn1_w_gate), bf(ffn1_w_up), bf(ffn1_w_down), vec(ffn1_post_g))
    ffn2 = (vec(ffn2_pre_g), bf(ffn2_w_gate), bf(ffn2_w_up), bf(ffn2_w_down), vec(ffn2_post_g))
    mix_g, w_in_bf, w_out_bf = vec(mix_pre_g), bf(w_in), bf(w_out)
    ssm_w_in, ssm_w_out, a_re, a_im = jax.vmap(_ssm_matrices)(
        ssm_lam_re, ssm_lam_im, ssm_log_dt, ssm_b_re, ssm_b_im, ssm_c_re, ssm_c_im)
    ssm_rest = (vec(ssm_d), bf(ssm_w_glu), vec(ssm_b_glu), vec(ssm_norm_g))
    ple = (bf(ple_w_up), bf(ple_w_gate), vec(ple_post_g))
    p_rows = p.reshape(depth, t, -1)

    h = x.reshape(t, d)
    rope = _rope(positions.reshape(t, 1).astype(F32), tm=ROPE_TILE)
    for i in range(depth):
        h, q, k, v, u = _pre(i, h, rope, *ffn1, mix_g, w_in_bf, d_attn=d_attn, tm=tm)
        attn = _attention(q.reshape(b, s, d_attn), k.reshape(b, s, d_attn), v.reshape(b, s, d_attn))
        ssm_n = _ssm(i, u.reshape(b, s, d_ssm), ssm_w_in, a_re, a_im, ssm_w_out, *ssm_rest, ts=ts)
        h = _post(i, h, attn.reshape(t, d_attn), ssm_n.reshape(t, d_ssm), p_rows, vec(attn_norm_g), w_out_bf,
                  vec(mix_post_g), *ffn2, *ple, tm=tm)
    return h.reshape(b, s, d)
```

```python
import functools

import numpy as np
import jax
import jax.numpy as jnp
from jax import lax
from jax.experimental import pallas as pl
from jax.experimental.pallas import tpu as pltpu

F32 = jnp.float32
BF16 = jnp.bfloat16

LANES = 128
SUBLANES = 8
MXU_DIM = 256

HEAD_DIM = 64
ROPE_DIM = HEAD_DIM // 4
ROPE_THETA = 500000.0
DILATED_PATTERNS = ((128, 1), (512, 4), (2048, 16))
BAND = 128
SSM_GROUP = 16
SSM_STATE = 64
SSM_CHUNK = 4
SSM_QUAD = 4
NORM_EPS = 1e-6
MASK_VALUE = -1e30

assert all(w // d == BAND for w, d in DILATED_PATTERNS)
DILATIONS = tuple(d for _, d in DILATED_PATTERNS)
ATTN_PLANES = 4
ATTN_UNROLL = 16

TOKEN_TILE = 512
ROPE_TILE = 2048
ROW_SPLITS = 2
SSM_TIME_TILE = 128
VMEM_LIMIT_BYTES = 56 * 1024 * 1024


def _rms_norm(x, g):
    ms = jnp.mean(x * x, axis=-1, keepdims=True)
    return x * lax.rsqrt(ms + NORM_EPS) * g


def _const_spec(shape):
    zeros = (0,) * len(shape)
    return pl.BlockSpec(shape, lambda *_: zeros, pipeline_mode=pl.Buffered(1))


def _layer_spec(layer, shape):
    zeros = (0,) * len(shape)
    return pl.BlockSpec((None,) + tuple(shape), lambda *_: (layer,) + zeros, pipeline_mode=pl.Buffered(1))


def _params(*sem):
    return pltpu.CompilerParams(dimension_semantics=sem, vmem_limit_bytes=VMEM_LIMIT_BYTES)


def _ffn_groups(xs, pre_ref, wg_ref, wu_ref, wd_ref, post_ref):
    xns = [_rms_norm(x, pre_ref[...]).astype(BF16) for x in xs]
    gs = [jnp.dot(xn, wg_ref[...], preferred_element_type=F32) for xn in xns]
    us = [jnp.dot(xn, wu_ref[...], preferred_element_type=F32) for xn in xns]
    mids = [(g * jax.nn.sigmoid(g) * u).astype(BF16) for g, u in zip(gs, us)]
    fs = [jnp.dot(mid, wd_ref[...], preferred_element_type=F32) for mid in mids]
    return [x + 0.5 * _rms_norm(f, post_ref[...]) for x, f in zip(xs, fs)]


def _row_groups(rows):
    sub = rows // ROW_SPLITS
    return [slice(i * sub, (i + 1) * sub) for i in range(ROW_SPLITS)]


def _rope_lane_tables():
    half = ROPE_DIM // 2
    inv_freq = ROPE_THETA ** (-jnp.arange(half, dtype=F32) * (2.0 / ROPE_DIM))
    lane = np.arange(LANES)
    in_head = lane % HEAD_DIM
    freq = jnp.where(in_head < ROPE_DIM, inv_freq[lane % half], 0.0).astype(F32)
    first = (in_head < half).astype(np.float32)
    second = ((in_head >= half) & (in_head < ROPE_DIM)).astype(np.float32)
    return freq.reshape(1, LANES), jnp.asarray(first).reshape(1, LANES), jnp.asarray(second).reshape(1, LANES)


def _rope_body(pos_ref, freq_ref, first_ref, second_ref, cos_ref, sin_first_ref, sin_second_ref):
    ang = pos_ref[...] * freq_ref[...]
    sin = jnp.sin(ang)
    cos_ref[...] = jnp.cos(ang)
    sin_first_ref[...] = -sin * first_ref[...]
    sin_second_ref[...] = sin * second_ref[...]


def _rope(pos, *, tm):
    t = pos.shape[0]
    freq, first, second = _rope_lane_tables()
    row = lambda w: pl.BlockSpec((tm, w), lambda i: (i, 0))
    lane_tab = _const_spec((1, LANES))
    return pl.pallas_call(
        _rope_body,
        grid=(t // tm,),
        in_specs=[row(1), lane_tab, lane_tab, lane_tab],
        out_specs=[row(LANES)] * 3,
        out_shape=[jax.ShapeDtypeStruct((t, LANES), F32)] * 3,
        compiler_params=_params("parallel"),
        name="rope",
    )(pos, freq, first, second)


def _pre_body(x_ref, cos_ref, sin_first_ref, sin_second_ref, pre_ref, wg_ref, wu_ref, wd_ref, post_ref,
              g_ref, w_ref, h_ref, q_ref, k_ref, v_ref, u_ref, *, d_attn):
    half = ROPE_DIM // 2
    scale = HEAD_DIM ** -0.5
    parts = _row_groups(x_ref.shape[0])
    hs = _ffn_groups([x_ref[rows, :] for rows in parts], pre_ref, wg_ref, wu_ref, wd_ref, post_ref)
    for rows, h in zip(parts, hs):
        h_ref[rows, :] = h
    hns = [_rms_norm(h, g_ref[...]).astype(BF16) for h in hs]
    projs = [jnp.dot(hn, w_ref[...], preferred_element_type=F32) for hn in hns]
    for rows, proj in zip(parts, projs):
        cos, sin_first, sin_second = cos_ref[rows, :], sin_first_ref[rows, :], sin_second_ref[rows, :]

        def rotary(t):
            return (t * cos + pltpu.roll(t, LANES - half, 1) * sin_first
                    + pltpu.roll(t, half, 1) * sin_second)

        for j in range(d_attn // LANES):
            sl = slice(j * LANES, (j + 1) * LANES)
            q_ref[rows, sl] = rotary(proj[:, j * LANES:(j + 1) * LANES]) * scale
            k_ref[rows, sl] = rotary(proj[:, d_attn + j * LANES:d_attn + (j + 1) * LANES])
        v_ref[rows, :] = proj[:, 2 * d_attn:3 * d_attn]
        u_ref[rows, :] = proj[:, 3 * d_attn:]


def _pre(layer, h, rope, pre_g, wg, wu, wd, post_g, mix_g, w_in, *, d_attn, tm):
    t, d = h.shape
    ff = wg.shape[2]
    n = w_in.shape[2]
    d_ssm = n - 3 * d_attn
    row = lambda w: pl.BlockSpec((tm, w), lambda i: (i, 0))
    per_layer = functools.partial(_layer_spec, layer)
    return pl.pallas_call(
        functools.partial(_pre_body, d_attn=d_attn),
        grid=(t // tm,),
        in_specs=[row(d), row(LANES), row(LANES), row(LANES), per_layer((1, d)), per_layer((d, ff)),
                  per_layer((d, ff)), per_layer((ff, d)), per_layer((1, d)), per_layer((1, d)), per_layer((d, n))],
        out_specs=[row(d), row(d_attn), row(d_attn), row(d_attn), row(d_ssm)],
        out_shape=[jax.ShapeDtypeStruct((t, d), F32)] + [jax.ShapeDtypeStruct((t, d_attn), F32)] * 3
        + [jax.ShapeDtypeStruct((t, d_ssm), F32)],
        compiler_params=_params("parallel"),
        name="pre",
    )(h, *rope, pre_g, wg, wu, wd, post_g, mix_g, w_in)


def _attn_body(q_ref, k_ref, v_ref, o_ref, planes_s, m_s, l_s, acc_s, *, seq):
    lane = lax.broadcasted_iota(jnp.int32, (BAND, LANES), 1)
    head0 = lane < HEAD_DIM
    qi = lax.broadcasted_iota(jnp.int32, (BAND, BAND), 0)
    kj = lax.broadcasted_iota(jnp.int32, (BAND, BAND), 1)
    cur_ok = kj <= qi
    prev_ok = kj >= qi

    plane_rows = seq // ATTN_PLANES
    ones = jnp.ones((2 * BAND, LANES), BF16)

    for t, ref in enumerate((q_ref, k_ref, v_ref)):
        for r in range(ATTN_PLANES):
            planes_s[t, pl.ds(r * plane_rows, plane_rows), :] = ref[pl.ds(r, plane_rows, stride=ATTN_PLANES), :]

    for pi, d in enumerate(DILATIONS):
        nb = seq // (BAND * d)
        step = max(d // ATTN_PLANES, 1)

        def rows(start):
            return pl.ds(start, BAND) if step == 1 else pl.ds(start, BAND, stride=step)

        def load(t, start):
            return (q_ref, k_ref, v_ref)[t][rows(start), :] if d == 1 else planes_s[t, rows(start), :]

        def block(n, carry):
            r = n // nb
            i = n - r * nb
            base = r if d == 1 else (r % ATTN_PLANES) * plane_rows + r // ATTN_PLANES
            start = base + i * (BAND * step)
            prev = jnp.maximum(start - BAND * step, base)
            q = load(0, start)
            zero = jnp.zeros_like(q)
            q2 = jnp.concatenate([jnp.where(head0, q, zero), jnp.where(head0, zero, q)], axis=0)
            kk = jnp.concatenate([load(1, prev), load(1, start)], axis=0)
            vv = jnp.concatenate([load(2, prev), load(2, start)], axis=0).astype(BF16)
            s = lax.dot_general(q2.astype(BF16), kk.astype(BF16), (((1,), (1,)), ((), ())),
                                preferred_element_type=F32)
            ok = jnp.concatenate([prev_ok & (i > 0), cur_ok], axis=1)
            ok = jnp.concatenate([ok, ok], axis=0)
            s = jnp.where(ok, s, MASK_VALUE)
            m = jnp.max(s, axis=-1, keepdims=True)
            e = jnp.exp(s - m).astype(BF16)
            o2 = jnp.dot(e, jnp.concatenate([vv, ones], axis=1), preferred_element_type=F32)
            m_s[pi, rows(start), :] = jnp.where(head0, m[:BAND], m[BAND:])
            l_s[pi, rows(start), :] = jnp.where(head0, o2[:BAND, LANES:], o2[BAND:, LANES:])
            acc_s[pi, rows(start), :] = jnp.where(head0, o2[:BAND, :LANES], o2[BAND:, :LANES])
            return carry

        lax.fori_loop(0, d * nb, block, 0, unroll=ATTN_UNROLL)

    blocks_per_plane = plane_rows // BAND

    def merge(n, carry):
        r = n // blocks_per_plane
        j = n - r * blocks_per_plane
        orig = pl.ds(r + j * (BAND * ATTN_PLANES), BAND, stride=ATTN_PLANES)
        plane = pl.ds(r * plane_rows + j * BAND, BAND)
        sel = lambda ref: [ref[0, orig, :]] + [ref[pi, plane, :] for pi in range(1, len(DILATIONS))]
        ms, ls, accs = sel(m_s), sel(l_s), sel(acc_s)
        m = functools.reduce(jnp.maximum, ms)
        ws = [jnp.exp(mp - m) for mp in ms]
        num = sum(w * a for w, a in zip(ws, accs))
        den = sum(w * l for w, l in zip(ws, ls))
        o_ref[orig, :] = num / den
        return carry

    lax.fori_loop(0, seq // BAND, merge, 0, unroll=2)


def _attention(q, k, v):
    b, s, d_attn = q.shape
    assert s % (BAND * max(DILATIONS)) == 0 and d_attn % LANES == 0
    assert all(d == 1 or d % ATTN_PLANES == 0 for d in DILATIONS)
    spec = pl.BlockSpec((None, s, LANES), lambda bi, hi: (bi, 0, hi))
    stats = pltpu.VMEM((len(DILATIONS), s, LANES), F32)
    return pl.pallas_call(
        functools.partial(_attn_body, seq=s),
        grid=(b, d_attn // LANES),
        in_specs=[spec, spec, spec],
        out_specs=spec,
        out_shape=jax.ShapeDtypeStruct((b, s, d_attn), F32),
        scratch_shapes=[pltpu.VMEM((3, s, LANES), F32), stats, stats, stats],
        compiler_params=_params("parallel", "parallel"),
        name="attention",
    )(q, k, v)


def _ssm_matrices(lam_re, lam_im, log_dt, b_re, b_im, c_re, c_im):
    g, p = lam_re.shape
    L, Q, H = SSM_CHUNK, SSM_QUAD, SSM_GROUP
    nq = g // Q
    dt = jnp.exp(log_dt)[:, None]
    mag = jnp.exp(lam_re * dt)
    ar = mag * jnp.cos(lam_im * dt)
    ai = mag * jnp.sin(lam_im * dt)
    wr, wi = ar - 1.0, ai
    den = lam_re * lam_re + lam_im * lam_im
    cr = (wr * lam_re + wi * lam_im) / den
    ci = (wi * lam_re - wr * lam_im) / den
    bbr = cr[..., None] * b_re - ci[..., None] * b_im
    bbi = cr[..., None] * b_im + ci[..., None] * b_re
    pw_r, pw_i = [jnp.ones_like(ar)], [jnp.zeros_like(ai)]
    for _ in range(L):
        pw_r.append(pw_r[-1] * ar - pw_i[-1] * ai)
        pw_i.append(pw_r[-2] * ai + pw_i[-1] * ar)
    pr, pi = jnp.stack(pw_r), jnp.stack(pw_i)

    own = np.eye(Q, dtype=np.float32)
    mask_hp = jnp.asarray(np.kron(own, np.ones((H, p), np.float32)))
    mask_ph = jnp.asarray(np.kron(own, np.ones((p, H), np.float32)))
    mask_hh = jnp.asarray(np.kron(own, np.ones((H, H), np.float32)))

    def spread(m, mask):
        lead = m.shape[:-3]
        a, b = m.shape[-2:]
        rows = m.reshape(lead + (nq, Q * a, b))
        return jnp.tile(rows, (1,) * (len(lead) + 2) + (Q,)) * mask

    bt_r, bt_i = jnp.swapaxes(bbr, 1, 2), jnp.swapaxes(bbi, 1, 2)
    nr, ni = pr[L - 1::-1][:, :, None, :], pi[L - 1::-1][:, :, None, :]
    win_r = spread(nr * bt_r - ni * bt_i, mask_hp)
    win_i = spread(nr * bt_i + ni * bt_r, mask_hp)
    w_in = jnp.concatenate([win_r, win_i], axis=-1)
    w_in = jnp.swapaxes(w_in, 0, 1).reshape(nq, L * Q * H, 2 * Q * p).astype(BF16)

    ct_r, ct_i = jnp.swapaxes(c_re, 1, 2), jnp.swapaxes(c_im, 1, 2)
    er = ct_r * pr[1:, :, :, None] - ct_i * pi[1:, :, :, None]
    ei = ct_r * pi[1:, :, :, None] + ct_i * pr[1:, :, :, None]
    zre = spread(er, mask_ph)
    zim = spread(-ei, mask_ph)
    e0r = ct_r * pr[:L, :, :, None] - ct_i * pi[:L, :, :, None]
    e0i = ct_r * pi[:L, :, :, None] + ct_i * pr[:L, :, :, None]
    knt = jnp.einsum("gpi,ngph->ngih", bbr, e0r) - jnp.einsum("gpi,ngph->ngih", bbi, e0i)
    kq = spread(knt, mask_hh)
    zero = jnp.zeros_like(kq[0])
    xin = jnp.stack([jnp.concatenate([kq[r - k] if r >= k else zero for r in range(L)], axis=-1)
                     for k in range(L)], axis=1)
    cols = lambda z: jnp.concatenate([z[r] for r in range(L)], axis=-1)
    w_out = jnp.concatenate([cols(zre), cols(zim), xin.reshape(nq, L * Q * H, L * Q * H)], axis=1).astype(BF16)
    al_re = jnp.broadcast_to(pw_r[L].reshape(1, g * p), (SUBLANES, g * p))
    al_im = jnp.broadcast_to(pw_i[L].reshape(1, g * p), (SUBLANES, g * p))
    return w_in, w_out, al_re, al_im


def _ssm_body(u_ref, win_ref, are_ref, aim_ref, wout_ref, dskip_ref, wglu_ref, bglu_ref, g_ref,
              o_ref, z_s, v_s, io_s, *, ts):
    nb = u_ref.shape[0]
    L = SSM_CHUNK
    nc = ts // L
    nq = win_ref.shape[0]
    half = LANES // 2
    sq = win_ref.shape[2] // 2
    n_state = nq * sq
    io_slabs = io_s.shape[0]
    lane = lax.broadcasted_iota(jnp.int32, (nc * nb, LANES), 1)
    low = lane < half

    @pl.when(pl.program_id(0) == 0)
    def _():
        z_s[...] = jnp.zeros(z_s.shape, F32)

    def batch_rows(bi):
        return pl.ds(bi, ts, stride=nb)

    for bi in range(nb):
        for j in range(io_slabs):
            io_s[j, batch_rows(bi), :] = u_ref[bi, :, j * LANES:(j + 1) * LANES]
    def step_rows(j, k):
        return jnp.concatenate([io_s[j, pl.ds((m * L + k) * nb, nb), :] for m in range(nc)], axis=0)

    us = [[step_rows(j, k) for k in range(L)] for j in range(io_slabs)]
    xq = []
    for j in range(io_slabs):
        rolled = [pltpu.roll(us[j][k], half, 1) for k in range(L)]
        for hi in range(2):
            pieces = []
            for k in range(0, L, 2):
                if hi == 0:
                    pieces.append(jnp.where(low, us[j][k], rolled[k + 1]))
                else:
                    pieces.append(jnp.where(low, rolled[k], us[j][k + 1]))
            xq.append(jnp.concatenate(pieces, axis=1).astype(BF16))

    for q in range(nq):
        v = jnp.dot(xq[q], win_ref[q], preferred_element_type=F32)
        v_s[:, q * sq:(q + 1) * sq] = v[:, :sq]
        v_s[:, n_state + q * sq:n_state + (q + 1) * sq] = v[:, sq:]

    for j in range(n_state // LANES):
        re = slice(j * LANES, (j + 1) * LANES)
        im = slice(n_state + j * LANES, n_state + (j + 1) * LANES)
        ar = are_ref[:, re]
        ai = aim_ref[:, re]
        zr = z_s[:, re]
        zi = z_s[:, im]
        for m in range(nc):
            row = slice(m * nb, (m + 1) * nb)
            nzr = ar * zr - ai * zi + v_s[row, re]
            nzi = ar * zi + ai * zr + v_s[row, im]
            v_s[row, re] = zr
            v_s[row, im] = zi
            zr, zi = nzr, nzi
        z_s[:, re] = zr
        z_s[:, im] = zi

    yq = []
    for q in range(nq):
        lhs = jnp.concatenate([v_s[:, q * sq:(q + 1) * sq].astype(BF16),
                               v_s[:, n_state + q * sq:n_state + (q + 1) * sq].astype(BF16), xq[q]], axis=1)
        yq.append(jnp.dot(lhs, wout_ref[q], preferred_element_type=F32))
    y_steps, u_steps = [], []
    for r in range(L):
        slabs = []
        for j in range(io_slabs):
            lo_src = yq[2 * j][:, (r // 2) * LANES:(r // 2 + 1) * LANES]
            hi_src = yq[2 * j + 1][:, (r // 2) * LANES:(r // 2 + 1) * LANES]
            if r % 2 == 0:
                slabs.append(jnp.where(low, lo_src, pltpu.roll(hi_src, half, 1)))
            else:
                slabs.append(jnp.where(low, pltpu.roll(lo_src, half, 1), hi_src))
        y_steps.append(jnp.concatenate(slabs, axis=1))
        u_steps.append(jnp.concatenate([us[j][r] for j in range(io_slabs)], axis=1))
    y = jnp.concatenate(y_steps, axis=0)
    u = jnp.concatenate(u_steps, axis=0)
    y = jax.nn.gelu(y + dskip_ref[...] * u)
    gate = jnp.dot(y.astype(BF16), wglu_ref[...], preferred_element_type=F32) + bglu_ref[...]
    y = _rms_norm(y * jax.nn.sigmoid(gate), g_ref[...])
    for r in range(L):
        for m in range(nc):
            src = (r * nc + m) * nb
            for j in range(io_slabs):
                io_s[j, pl.ds((m * L + r) * nb, nb), :] = y[src:src + nb, j * LANES:(j + 1) * LANES]
    for bi in range(nb):
        o_ref[bi] = jnp.concatenate([io_s[j, batch_rows(bi), :] for j in range(io_slabs)],
                                    axis=1).astype(o_ref.dtype)


def _ssm(layer, u, w_in, a_re, a_im, w_out, d_skip, w_glu, b_glu, norm_g, *, ts):
    b, s, d_ssm = u.shape
    assert b == SUBLANES, "the S5 scan keeps the batch on the sublane axis"
    assert SSM_CHUNK * SSM_QUAD * SSM_GROUP == MXU_DIM and SSM_QUAD * SSM_GROUP * 2 == LANES and SSM_CHUNK % 2 == 0
    n_state = a_re.shape[2]
    tile = pl.BlockSpec((b, ts, d_ssm), lambda i: (0, i, 0))
    per_layer = functools.partial(_layer_spec, layer)
    return pl.pallas_call(
        functools.partial(_ssm_body, ts=ts),
        grid=(s // ts,),
        in_specs=[tile, per_layer(w_in.shape[1:]), per_layer(a_re.shape[1:]), per_layer(a_im.shape[1:]),
                  per_layer(w_out.shape[1:]), per_layer((1, d_ssm)), per_layer((d_ssm, d_ssm)),
                  per_layer((1, d_ssm)), per_layer((1, d_ssm))],
        out_specs=tile,
        out_shape=jax.ShapeDtypeStruct((b, s, d_ssm), BF16),
        scratch_shapes=[pltpu.VMEM((b, 2 * n_state), F32),
                        pltpu.VMEM((ts // SSM_CHUNK * b, 2 * n_state), F32),
                        pltpu.VMEM((d_ssm // LANES, ts * b, LANES), F32)],
        compiler_params=_params("arbitrary"),
        name="ssm",
    )(u, w_in, a_re, a_im, w_out, d_skip, w_glu, b_glu, norm_g)


def _post_body(h_ref, attn_ref, ssm_ref, p_ref, ga_ref, wa_ref, ws_ref, mixpost_ref,
               pre_ref, wg_ref, wu_ref, wd_ref, post_ref, wup_ref, wgate_ref, plepost_ref, o_ref):
    parts = _row_groups(h_ref.shape[0])
    ans = [_rms_norm(attn_ref[rows, :], ga_ref[...]).astype(BF16) for rows in parts]
    mixes = [jnp.dot(an, wa_ref[...], preferred_element_type=F32)
             + jnp.dot(ssm_ref[rows, :], ws_ref[...], preferred_element_type=F32) for an, rows in zip(ans, parts)]
    xs = [h_ref[rows, :] + _rms_norm(mix, mixpost_ref[...]) for rows, mix in zip(parts, mixes)]
    hs = _ffn_groups(xs, pre_ref, wg_ref, wu_ref, wd_ref, post_ref)
    ups = [jnp.dot(p_ref[rows, :].astype(BF16), wup_ref[...], preferred_element_type=F32) for rows in parts]
    gates = [jnp.dot(h.astype(BF16), wgate_ref[...], preferred_element_type=F32) for h in hs]
    for rows, h, up, gate in zip(parts, hs, ups, gates):
        o_ref[rows, :] = h + _rms_norm(up * jax.nn.sigmoid(gate), plepost_ref[...])


def _post(layer, h, attn, ssm_n, p, attn_g, w_out, mix_post_g, pre_g, wg, wu, wd, post_g, w_up, w_gate,
          ple_post_g, *, tm):
    t, d = h.shape
    da, ds, dp = attn.shape[1], ssm_n.shape[1], p.shape[2]
    ff = wg.shape[2]
    row = lambda w: pl.BlockSpec((tm, w), lambda i: (i, 0))
    per_layer = functools.partial(_layer_spec, layer)
    vec = per_layer((1, d))
    w_attn = pl.BlockSpec((None, da, d), lambda i: (layer, 0, 0), pipeline_mode=pl.Buffered(1))
    w_ssm = pl.BlockSpec((None, ds, d), lambda i: (layer, da // ds, 0), pipeline_mode=pl.Buffered(1))
    return pl.pallas_call(
        _post_body,
        grid=(t // tm,),
        in_specs=[row(d), row(da), row(ds), pl.BlockSpec((None, tm, dp), lambda i: (layer, i, 0)),
                  per_layer((1, da)), w_attn, w_ssm, vec, vec, per_layer((d, ff)), per_layer((d, ff)),
                  per_layer((ff, d)), vec, per_layer((dp, d)), per_layer((d, d)), vec],
        out_specs=row(d),
        out_shape=jax.ShapeDtypeStruct((t, d), F32),
        compiler_params=_params("parallel"),
        name="post",
    )(h, attn, ssm_n, p, attn_g, w_out, w_out, mix_post_g, pre_g, wg, wu, wd, post_g, w_up, w_gate, ple_post_g)


def kernel(x, p, positions, ffn1_pre_g, ffn1_w_gate, ffn1_w_up, ffn1_w_down, ffn1_post_g, mix_pre_g, w_in, attn_norm_g, ssm_lam_re, ssm_lam_im, ssm_log_dt, ssm_b_re, ssm_b_im, ssm_c_re, ssm_c_im, ssm_d, ssm_w_glu, ssm_b_glu, ssm_norm_g, w_out, mix_post_g, ffn2_pre_g, ffn2_w_gate, ffn2_w_up, ffn2_w_down, ffn2_post_g, ple_w_up, ple_w_gate, ple_post_g):
    b, s, d = x.shape
    depth = p.shape[0]
    t = b * s
    d_attn = attn_norm_g.shape[1]
    d_ssm = ssm_norm_g.shape[1]
    assert d_attn % d_ssm == 0
    tm = min(TOKEN_TILE, t)
    ts = min(SSM_TIME_TILE, s)
    vec = lambda a: a.reshape(depth, 1, -1)
    bf = lambda a: a.astype(BF16)

    ffn1 = (vec(ffn1_pre_g), bf(ffn1_w_gate), bf(ffn1_w_up), bf(ffn1_w_down), vec(ffn1_post_g))
    ffn2 = (vec(ffn2_pre_g), bf(ffn2_w_gate), bf(ffn2_w_up), bf(ffn2_w_down), vec(ffn2_post_g))
    mix_g, w_in_bf, w_out_bf = vec(mix_pre_g), bf(w_in), bf(w_out)
    ssm_w_in, ssm_w_out, a_re, a_im = jax.vmap(_ssm_matrices)(
        ssm_lam_re, ssm_lam_im, ssm_log_dt, ssm_b_re, ssm_b_im, ssm_c_re, ssm_c_im)
    ssm_rest = (vec(ssm_d), bf(ssm_w_glu), vec(ssm_b_glu), vec(ssm_norm_g))
    ple = (bf(ple_w_up), bf(ple_w_gate), vec(ple_post_g))
    p_rows = p.reshape(depth, t, -1)

    h = x.reshape(t, d)
    rope = _rope(positions.reshape(t, 1).astype(F32), tm=ROPE_TILE)
    for i in range(depth):
        h, q, k, v, u = _pre(i, h, rope, *ffn1, mix_g, w_in_bf, d_attn=d_attn, tm=tm)
        attn = _attention(q.reshape(b, s, d_attn), k.reshape(b, s, d_attn), v.reshape(b, s, d_attn))
        ssm_n = _ssm(i, u.reshape(b, s, d_ssm), ssm_w_in, a_re, a_im, ssm_w_out, *ssm_rest, ts=ts)
        h = _post(i, h, attn.reshape(t, d_attn), ssm_n.reshape(t, d_ssm), p_rows, vec(attn_norm_g), w_out_bf,
                  vec(mix_post_g), *ffn2, *ple, tm=tm)
    return h.reshape(b, s, d)
```

```python
import functools

import numpy as np
import jax
import jax.numpy as jnp
from jax import lax
from jax.experimental import pallas as pl
from jax.experimental.pallas import tpu as pltpu

F32 = jnp.float32
BF16 = jnp.bfloat16

LANES = 128
SUBLANES = 8
MXU_DIM = 256

HEAD_DIM = 64
ROPE_DIM = HEAD_DIM // 4
ROPE_THETA = 500000.0
DILATED_PATTERNS = ((128, 1), (512, 4), (2048, 16))
BAND = 128
SSM_GROUP = 16
SSM_STATE = 64
SSM_CHUNK = 4
SSM_QUAD = 4
NORM_EPS = 1e-6
MASK_VALUE = -1e30

assert all(w // d == BAND for w, d in DILATED_PATTERNS)
DILATIONS = tuple(d for _, d in DILATED_PATTERNS)
ATTN_PLANES = 4
ATTN_UNROLL = 32

TOKEN_TILE = 512
ROPE_TILE = 2048
ROW_SPLITS = 2
SSM_TIME_TILE = 128
VMEM_LIMIT_BYTES = 56 * 1024 * 1024


def _rms_norm(x, g):
    ms = jnp.mean(x * x, axis=-1, keepdims=True)
    return x * lax.rsqrt(ms + NORM_EPS) * g


def _const_spec(shape):
    zeros = (0,) * len(shape)
    return pl.BlockSpec(shape, lambda *_: zeros, pipeline_mode=pl.Buffered(1))


def _layer_spec(layer, shape):
    zeros = (0,) * len(shape)
    return pl.BlockSpec((None,) + tuple(shape), lambda *_: (layer,) + zeros, pipeline_mode=pl.Buffered(1))


def _params(*sem):
    return pltpu.CompilerParams(dimension_semantics=sem, vmem_limit_bytes=VMEM_LIMIT_BYTES)


def _ffn_groups(xs, pre_ref, wg_ref, wu_ref, wd_ref, post_ref):
    xns = [_rms_norm(x, pre_ref[...]).astype(BF16) for x in xs]
    gs = [jnp.dot(xn, wg_ref[...], preferred_element_type=F32) for xn in xns]
    us = [jnp.dot(xn, wu_ref[...], preferred_element_type=F32) for xn in xns]
    mids = [(g * jax.nn.sigmoid(g) * u).astype(BF16) for g, u in zip(gs, us)]
    fs = [jnp.dot(mid, wd_ref[...], preferred_element_type=F32) for mid in mids]
    return [x + 0.5 * _rms_norm(f, post_ref[...]) for x, f in zip(xs, fs)]


def _row_groups(rows):
    sub = rows // ROW_SPLITS
    return [slice(i * sub, (i + 1) * sub) for i in range(ROW_SPLITS)]


def _rope_lane_tables():
    half = ROPE_DIM // 2
    inv_freq = ROPE_THETA ** (-jnp.arange(half, dtype=F32) * (2.0 / ROPE_DIM))
    lane = np.arange(LANES)
    in_head = lane % HEAD_DIM
    freq = jnp.where(in_head < ROPE_DIM, inv_freq[lane % half], 0.0).astype(F32)
    first = (in_head < half).astype(np.float32)
    second = ((in_head >= half) & (in_head < ROPE_DIM)).astype(np.float32)
    return freq.reshape(1, LANES), jnp.asarray(first).reshape(1, LANES), jnp.asarray(second).reshape(1, LANES)


def _rope_body(pos_ref, freq_ref, first_ref, second_ref, cos_ref, sin_first_ref, sin_second_ref):
    ang = pos_ref[...] * freq_ref[...]
    sin = jnp.sin(ang)
    cos_ref[...] = jnp.cos(ang)
    sin_first_ref[...] = -sin * first_ref[...]
    sin_second_ref[...] = sin * second_ref[...]


def _rope(pos, *, tm):
    t = pos.shape[0]
    freq, first, second = _rope_lane_tables()
    row = lambda w: pl.BlockSpec((tm, w), lambda i: (i, 0))
    lane_tab = _const_spec((1, LANES))
    return pl.pallas_call(
        _rope_body,
        grid=(t // tm,),
        in_specs=[row(1), lane_tab, lane_tab, lane_tab],
        out_specs=[row(LANES)] * 3,
        out_shape=[jax.ShapeDtypeStruct((t, LANES), F32)] * 3,
        compiler_params=_params("parallel"),
        name="rope",
    )(pos, freq, first, second)


def _pre_body(x_ref, cos_ref, sin_first_ref, sin_second_ref, pre_ref, wg_ref, wu_ref, wd_ref, post_ref,
              g_ref, w_ref, h_ref, q_ref, k_ref, v_ref, u_ref, *, d_attn):
    half = ROPE_DIM // 2
    scale = HEAD_DIM ** -0.5
    parts = _row_groups(x_ref.shape[0])
    hs = _ffn_groups([x_ref[rows, :] for rows in parts], pre_ref, wg_ref, wu_ref, wd_ref, post_ref)
    for rows, h in zip(parts, hs):
        h_ref[rows, :] = h
    hns = [_rms_norm(h, g_ref[...]).astype(BF16) for h in hs]
    projs = [jnp.dot(hn, w_ref[...], preferred_element_type=F32) for hn in hns]
    for rows, proj in zip(parts, projs):
        cos, sin_first, sin_second = cos_ref[rows, :], sin_first_ref[rows, :], sin_second_ref[rows, :]

        def rotary(t):
            return (t * cos + pltpu.roll(t, LANES - half, 1) * sin_first
                    + pltpu.roll(t, half, 1) * sin_second)

        for j in range(d_attn // LANES):
            sl = slice(j * LANES, (j + 1) * LANES)
            q_ref[rows, sl] = rotary(proj[:, j * LANES:(j + 1) * LANES]) * scale
            k_ref[rows, sl] = rotary(proj[:, d_attn + j * LANES:d_attn + (j + 1) * LANES])
        v_ref[rows, :] = proj[:, 2 * d_attn:3 * d_attn]
        u_ref[rows, :] = proj[:, 3 * d_attn:]


def _pre(layer, h, rope, pre_g, wg, wu, wd, post_g, mix_g, w_in, *, d_attn, tm):
    t, d = h.shape
    ff = wg.shape[2]
    n = w_in.shape[2]
    d_ssm = n - 3 * d_attn
    row = lambda w: pl.BlockSpec((tm, w), lambda i: (i, 0))
    per_layer = functools.partial(_layer_spec, layer)
    return pl.pallas_call(
        functools.partial(_pre_body, d_attn=d_attn),
        grid=(t // tm,),
        in_specs=[row(d), row(LANES), row(LANES), row(LANES), per_layer((1, d)), per_layer((d, ff)),
                  per_layer((d, ff)), per_layer((ff, d)), per_layer((1, d)), per_layer((1, d)), per_layer((d, n))],
        out_specs=[row(d), row(d_attn), row(d_attn), row(d_attn), row(d_ssm)],
        out_shape=[jax.ShapeDtypeStruct((t, d), F32)] + [jax.ShapeDtypeStruct((t, d_attn), F32)] * 3
        + [jax.ShapeDtypeStruct((t, d_ssm), F32)],
        compiler_params=_params("parallel"),
        name="pre",
    )(h, *rope, pre_g, wg, wu, wd, post_g, mix_g, w_in)


def _attn_body(q_ref, k_ref, v_ref, o_ref, planes_s, m_s, l_s, acc_s, *, seq):
    lane = lax.broadcasted_iota(jnp.int32, (BAND, LANES), 1)
    head0 = lane < HEAD_DIM
    qi = lax.broadcasted_iota(jnp.int32, (BAND, BAND), 0)
    kj = lax.broadcasted_iota(jnp.int32, (BAND, BAND), 1)
    cur_ok = kj <= qi
    prev_ok = kj >= qi

    plane_rows = seq // ATTN_PLANES
    ones = jnp.ones((2 * BAND, LANES), BF16)

    for t, ref in enumerate((q_ref, k_ref, v_ref)):
        for r in range(ATTN_PLANES):
            planes_s[t, pl.ds(r * plane_rows, plane_rows), :] = ref[pl.ds(r, plane_rows, stride=ATTN_PLANES), :]

    for pi, d in enumerate(DILATIONS):
        nb = seq // (BAND * d)
        step = max(d // ATTN_PLANES, 1)

        def rows(start):
            return pl.ds(start, BAND) if step == 1 else pl.ds(start, BAND, stride=step)

        def load(t, start):
            return (q_ref, k_ref, v_ref)[t][rows(start), :] if d == 1 else planes_s[t, rows(start), :]

        def block(n, carry):
            r = n // nb
            i = n - r * nb
            base = r if d == 1 else (r % ATTN_PLANES) * plane_rows + r // ATTN_PLANES
            start = base + i * (BAND * step)
            prev = jnp.maximum(start - BAND * step, base)
            q = load(0, start)
            zero = jnp.zeros_like(q)
            q2 = jnp.concatenate([jnp.where(head0, q, zero), jnp.where(head0, zero, q)], axis=0)
            kk = jnp.concatenate([load(1, prev), load(1, start)], axis=0)
            vv = jnp.concatenate([load(2, prev), load(2, start)], axis=0).astype(BF16)
            s = lax.dot_general(q2.astype(BF16), kk.astype(BF16), (((1,), (1,)), ((), ())),
                                preferred_element_type=F32)
            ok = jnp.concatenate([prev_ok & (i > 0), cur_ok], axis=1)
            ok = jnp.concatenate([ok, ok], axis=0)
            s = jnp.where(ok, s, MASK_VALUE)
            m = jnp.max(s, axis=-1, keepdims=True)
            e = jnp.exp(s - m).astype(BF16)
            o2 = jnp.dot(e, jnp.concatenate([vv, ones], axis=1), preferred_element_type=F32)
            m_s[pi, rows(start), :] = jnp.where(head0, m[:BAND], m[BAND:])
            l_s[pi, rows(start), :] = jnp.where(head0, o2[:BAND, LANES:], o2[BAND:, LANES:])
            acc_s[pi, rows(start), :] = jnp.where(head0, o2[:BAND, :LANES], o2[BAND:, :LANES])
            return carry

        lax.fori_loop(0, d * nb, block, 0, unroll=ATTN_UNROLL)

    blocks_per_plane = plane_rows // BAND

    def merge(n, carry):
        r = n // blocks_per_plane
        j = n - r * blocks_per_plane
        orig = pl.ds(r + j * (BAND * ATTN_PLANES), BAND, stride=ATTN_PLANES)
        plane = pl.ds(r * plane_rows + j * BAND, BAND)
        sel = lambda ref: [ref[0, orig, :]] + [ref[pi, plane, :] for pi in range(1, len(DILATIONS))]
        ms, ls, accs = sel(m_s), sel(l_s), sel(acc_s)
        m = functools.reduce(jnp.maximum, ms)
        ws = [jnp.exp(mp - m) for mp in ms]
        num = sum(w * a for w, a in zip(ws, accs))
        den = sum(w * l for w, l in zip(ws, ls))
        o_ref[orig, :] = num / den
        return carry

    lax.fori_loop(0, seq // BAND, merge, 0, unroll=2)


def _attention(q, k, v):
    b, s, d_attn = q.shape
    assert s % (BAND * max(DILATIONS)) == 0 and d_attn % LANES == 0
    assert all(d == 1 or d % ATTN_PLANES == 0 for d in DILATIONS)
    spec = pl.BlockSpec((None, s, LANES), lambda bi, hi: (bi, 0, hi))
    stats = pltpu.VMEM((len(DILATIONS), s, LANES), F32)
    return pl.pallas_call(
        functools.partial(_attn_body, seq=s),
        grid=(b, d_attn // LANES),
        in_specs=[spec, spec, spec],
        out_specs=spec,
        out_shape=jax.ShapeDtypeStruct((b, s, d_attn), F32),
        scratch_shapes=[pltpu.VMEM((3, s, LANES), F32), stats, stats, stats],
        compiler_params=_params("parallel", "parallel"),
        name="attention",
    )(q, k, v)


def _ssm_matrices(lam_re, lam_im, log_dt, b_re, b_im, c_re, c_im):
    g, p = lam_re.shape
    L, Q, H = SSM_CHUNK, SSM_QUAD, SSM_GROUP
    nq = g // Q
    dt = jnp.exp(log_dt)[:, None]
    mag = jnp.exp(lam_re * dt)
    ar = mag * jnp.cos(lam_im * dt)
    ai = mag * jnp.sin(lam_im * dt)
    wr, wi = ar - 1.0, ai
    den = lam_re * lam_re + lam_im * lam_im
    cr = (wr * lam_re + wi * lam_im) / den
    ci = (wi * lam_re - wr * lam_im) / den
    bbr = cr[..., None] * b_re - ci[..., None] * b_im
    bbi = cr[..., None] * b_im + ci[..., None] * b_re
    pw_r, pw_i = [jnp.ones_like(ar)], [jnp.zeros_like(ai)]
    for _ in range(L):
        pw_r.append(pw_r[-1] * ar - pw_i[-1] * ai)
        pw_i.append(pw_r[-2] * ai + pw_i[-1] * ar)
    pr, pi = jnp.stack(pw_r), jnp.stack(pw_i)

    own = np.eye(Q, dtype=np.float32)
    mask_hp = jnp.asarray(np.kron(own, np.ones((H, p), np.float32)))
    mask_ph = jnp.asarray(np.kron(own, np.ones((p, H), np.float32)))
    mask_hh = jnp.asarray(np.kron(own, np.ones((H, H), np.float32)))

    def spread(m, mask):
        lead = m.shape[:-3]
        a, b = m.shape[-2:]
        rows = m.reshape(lead + (nq, Q * a, b))
        return jnp.tile(rows, (1,) * (len(lead) + 2) + (Q,)) * mask

    bt_r, bt_i = jnp.swapaxes(bbr, 1, 2), jnp.swapaxes(bbi, 1, 2)
    nr, ni = pr[L - 1::-1][:, :, None, :], pi[L - 1::-1][:, :, None, :]
    win_r = spread(nr * bt_r - ni * bt_i, mask_hp)
    win_i = spread(nr * bt_i + ni * bt_r, mask_hp)
    w_in = jnp.concatenate([win_r, win_i], axis=-1)
    w_in = jnp.swapaxes(w_in, 0, 1).reshape(nq, L * Q * H, 2 * Q * p).astype(BF16)

    ct_r, ct_i = jnp.swapaxes(c_re, 1, 2), jnp.swapaxes(c_im, 1, 2)
    er = ct_r * pr[1:, :, :, None] - ct_i * pi[1:, :, :, None]
    ei = ct_r * pi[1:, :, :, None] + ct_i * pr[1:, :, :, None]
    zre = spread(er, mask_ph)
    zim = spread(-ei, mask_ph)
    e0r = ct_r * pr[:L, :, :, None] - ct_i * pi[:L, :, :, None]
    e0i = ct_r * pi[:L, :, :, None] + ct_i * pr[:L, :, :, None]
    knt = jnp.einsum("gpi,ngph->ngih", bbr, e0r) - jnp.einsum("gpi,ngph->ngih", bbi, e0i)
    kq = spread(knt, mask_hh)
    zero = jnp.zeros_like(kq[0])
    xin = jnp.stack([jnp.concatenate([kq[r - k] if r >= k else zero for r in range(L)], axis=-1)
                     for k in range(L)], axis=1)
    cols = lambda z: jnp.concatenate([z[r] for r in range(L)], axis=-1)
    w_out = jnp.concatenate([cols(zre), cols(zim), xin.reshape(nq, L * Q * H, L * Q * H)], axis=1).astype(BF16)
    al_re = jnp.broadcast_to(pw_r[L].reshape(1, g * p), (SUBLANES, g * p))
    al_im = jnp.broadcast_to(pw_i[L].reshape(1, g * p), (SUBLANES, g * p))
    return w_in, w_out, al_re, al_im


def _ssm_body(u_ref, win_ref, are_ref, aim_ref, wout_ref, dskip_ref, wglu_ref, bglu_ref, g_ref,
              o_ref, z_s, v_s, io_s, *, ts):
    nb = u_ref.shape[0]
    L = SSM_CHUNK
    nc = ts // L
    nq = win_ref.shape[0]
    half = LANES // 2
    sq = win_ref.shape[2] // 2
    n_state = nq * sq
    io_slabs = io_s.shape[0]
    lane = lax.broadcasted_iota(jnp.int32, (nc * nb, LANES), 1)
    low = lane < half

    @pl.when(pl.program_id(0) == 0)
    def _():
        z_s[...] = jnp.zeros(z_s.shape, F32)

    def batch_rows(bi):
        return pl.ds(bi, ts, stride=nb)

    for bi in range(nb):
        for j in range(io_slabs):
            io_s[j, batch_rows(bi), :] = u_ref[bi, :, j * LANES:(j + 1) * LANES]
    def step_rows(j, k):
        return jnp.concatenate([io_s[j, pl.ds((m * L + k) * nb, nb), :] for m in range(nc)], axis=0)

    us = [[step_rows(j, k) for k in range(L)] for j in range(io_slabs)]
    xq = []
    for j in range(io_slabs):
        rolled = [pltpu.roll(us[j][k], half, 1) for k in range(L)]
        for hi in range(2):
            pieces = []
            for k in range(0, L, 2):
                if hi == 0:
                    pieces.append(jnp.where(low, us[j][k], rolled[k + 1]))
                else:
                    pieces.append(jnp.where(low, rolled[k], us[j][k + 1]))
            xq.append(jnp.concatenate(pieces, axis=1).astype(BF16))

    for q in range(nq):
        v = jnp.dot(xq[q], win_ref[q], preferred_element_type=F32)
        v_s[:, q * sq:(q + 1) * sq] = v[:, :sq]
        v_s[:, n_state + q * sq:n_state + (q + 1) * sq] = v[:, sq:]

    for j in range(n_state // LANES):
        re = slice(j * LANES, (j + 1) * LANES)
        im = slice(n_state + j * LANES, n_state + (j + 1) * LANES)
        ar = are_ref[:, re]
        ai = aim_ref[:, re]
        zr = z_s[:, re]
        zi = z_s[:, im]
        for m in range(nc):
            row = slice(m * nb, (m + 1) * nb)
            nzr = ar * zr - ai * zi + v_s[row, re]
            nzi = ar * zi + ai * zr + v_s[row, im]
            v_s[row, re] = zr
            v_s[row, im] = zi
            zr, zi = nzr, nzi
        z_s[:, re] = zr
        z_s[:, im] = zi

    yq = []
    for q in range(nq):
        lhs = jnp.concatenate([v_s[:, q * sq:(q + 1) * sq].astype(BF16),
                               v_s[:, n_state + q * sq:n_state + (q + 1) * sq].astype(BF16), xq[q]], axis=1)
        yq.append(jnp.dot(lhs, wout_ref[q], preferred_element_type=F32))
    y_steps, u_steps = [], []
    for r in range(L):
        slabs = []
        for j in range(io_slabs):
            lo_src = yq[2 * j][:, (r // 2) * LANES:(r // 2 + 1) * LANES]
            hi_src = yq[2 * j + 1][:, (r // 2) * LANES:(r // 2 + 1) * LANES]
            if r % 2 == 0:
                slabs.append(jnp.where(low, lo_src, pltpu.roll(hi_src, half, 1)))
            else:
                slabs.append(jnp.where(low, pltpu.roll(lo_src, half, 1), hi_src))
        y_steps.append(jnp.concatenate(slabs, axis=1))
        u_steps.append(jnp.concatenate([us[j][r] for j in range(io_slabs)], axis=1))
    y = jnp.concatenate(y_steps, axis=0)
    u = jnp.concatenate(u_steps, axis=0)
    y = jax.nn.gelu(y + dskip_ref[...] * u)
    gate = jnp.dot(y.astype(BF16), wglu_ref[...], preferred_element_type=F32) + bglu_ref[...]
    y = _rms_norm(y * jax.nn.sigmoid(gate), g_ref[...])
    for r in range(L):
        for m in range(nc):
            src = (r * nc + m) * nb
            for j in range(io_slabs):
                io_s[j, pl.ds((m * L + r) * nb, nb), :] = y[src:src + nb, j * LANES:(j + 1) * LANES]
    for bi in range(nb):
        o_ref[bi] = jnp.concatenate([io_s[j, batch_rows(bi), :] for j in range(io_slabs)],
                                    axis=1).astype(o_ref.dtype)


def _ssm(layer, u, w_in, a_re, a_im, w_out, d_skip, w_glu, b_glu, norm_g, *, ts):
    b, s, d_ssm = u.shape
    assert b == SUBLANES, "the S5 scan keeps the batch on the sublane axis"
    assert SSM_CHUNK * SSM_QUAD * SSM_GROUP == MXU_DIM and SSM_QUAD * SSM_GROUP * 2 == LANES and SSM_CHUNK % 2 == 0
    n_state = a_re.shape[2]
    tile = pl.BlockSpec((b, ts, d_ssm), lambda i: (0, i, 0))
    per_layer = functools.partial(_layer_spec, layer)
    return pl.pallas_call(
        functools.partial(_ssm_body, ts=ts),
        grid=(s // ts,),
        in_specs=[tile, per_layer(w_in.shape[1:]), per_layer(a_re.shape[1:]), per_layer(a_im.shape[1:]),
                  per_layer(w_out.shape[1:]), per_layer((1, d_ssm)), per_layer((d_ssm, d_ssm)),
                  per_layer((1, d_ssm)), per_layer((1, d_ssm))],
        out_specs=tile,
        out_shape=jax.ShapeDtypeStruct((b, s, d_ssm), BF16),
        scratch_shapes=[pltpu.VMEM((b, 2 * n_state), F32),
                        pltpu.VMEM((ts // SSM_CHUNK * b, 2 * n_state), F32),
                        pltpu.VMEM((d_ssm // LANES, ts * b, LANES), F32)],
        compiler_params=_params("arbitrary"),
        name="ssm",
    )(u, w_in, a_re, a_im, w_out, d_skip, w_glu, b_glu, norm_g)


def _post_body(h_ref, attn_ref, ssm_ref, p_ref, ga_ref, wa_ref, ws_ref, mixpost_ref,
               pre_ref, wg_ref, wu_ref, wd_ref, post_ref, wup_ref, wgate_ref, plepost_ref, o_ref):
    parts = _row_groups(h_ref.shape[0])
    ans = [_rms_norm(attn_ref[rows, :], ga_ref[...]).astype(BF16) for rows in parts]
    mixes = [jnp.dot(an, wa_ref[...], preferred_element_type=F32)
             + jnp.dot(ssm_ref[rows, :], ws_ref[...], preferred_element_type=F32) for an, rows in zip(ans, parts)]
    xs = [h_ref[rows, :] + _rms_norm(mix, mixpost_ref[...]) for rows, mix in zip(parts, mixes)]
    hs = _ffn_groups(xs, pre_ref, wg_ref, wu_ref, wd_ref, post_ref)
    ups = [jnp.dot(p_ref[rows, :].astype(BF16), wup_ref[...], preferred_element_type=F32) for rows in parts]
    gates = [jnp.dot(h.astype(BF16), wgate_ref[...], preferred_element_type=F32) for h in hs]
    for rows, h, up, gate in zip(parts, hs, ups, gates):
        o_ref[rows, :] = h + _rms_norm(up * jax.nn.sigmoid(gate), plepost_ref[...])


def _post(layer, h, attn, ssm_n, p, attn_g, w_out, mix_post_g, pre_g, wg, wu, wd, post_g, w_up, w_gate,
          ple_post_g, *, tm):
    t, d = h.shape
    da, ds, dp = attn.shape[1], ssm_n.shape[1], p.shape[2]
    ff = wg.shape[2]
    row = lambda w: pl.BlockSpec((tm, w), lambda i: (i, 0))
    per_layer = functools.partial(_layer_spec, layer)
    vec = per_layer((1, d))
    w_attn = pl.BlockSpec((None, da, d), lambda i: (layer, 0, 0), pipeline_mode=pl.Buffered(1))
    w_ssm = pl.BlockSpec((None, ds, d), lambda i: (layer, da // ds, 0), pipeline_mode=pl.Buffered(1))
    return pl.pallas_call(
        _post_body,
        grid=(t // tm,),
        in_specs=[row(d), row(da), row(ds), pl.BlockSpec((None, tm, dp), lambda i: (layer, i, 0)),
                  per_layer((1, da)), w_attn, w_ssm, vec, vec, per_layer((d, ff)), per_layer((d, ff)),
                  per_layer((ff, d)), vec, per_layer((dp, d)), per_layer((d, d)), vec],
        out_specs=row(d),
        out_shape=jax.ShapeDtypeStruct((t, d), F32),
        compiler_params=_params("parallel"),
        name="post",
    )(h, attn, ssm_n, p, attn_g, w_out, w_out, mix_post_g, pre_g, wg, wu, wd, post_g, w_up, w_gate, ple_post_g)


def kernel(x, p, positions, ffn1_pre_g, ffn1_w_gate, ffn1_w_up, ffn1_w_down, ffn1_post_g, mix_pre_g, w_in, attn_norm_g, ssm_lam_re, ssm_lam_im, ssm_log_dt, ssm_b_re, ssm_b_im, ssm_c_re, ssm_c_im, ssm_d, ssm_w_glu, ssm_b_glu, ssm_norm_g, w_out, mix_post_g, ffn2_pre_g, ffn2_w_gate, ffn2_w_up, ffn2_w_down, ffn2_post_g, ple_w_up, ple_w_gate, ple_post_g):
    b, s, d = x.shape
    depth = p.shape[0]
    t = b * s
    d_attn = attn_norm_g.shape[1]
    d_ssm = ssm_norm_g.shape[1]
    assert d_attn % d_ssm == 0
    tm = min(TOKEN_TILE, t)
    ts = min(SSM_TIME_TILE, s)
    vec = lambda a: a.reshape(depth, 1, -1)
    bf = lambda a: a.astype(BF16)

    ffn1 = (vec(ffn1_pre_g), bf(ffn1_w_gate), bf(ffn1_w_up), bf(ffn1_w_down), vec(ffn1_post_g))
    ffn2 = (vec(ffn2_pre_g), bf(ffn2_w_gate), bf(ffn2_w_up), bf(ffn2_w_down), vec(ffn2_post_g))
    mix_g, w_in_bf, w_out_bf = vec(mix_pre_g), bf(w_in), bf(w_out)
    ssm_w_in, ssm_w_out, a_re, a_im = jax.vmap(_ssm_matrices)(
        ssm_lam_re, ssm_lam_im, ssm_log_dt, ssm_b_re, ssm_b_im, ssm_c_re, ssm_c_im)
    ssm_rest = (vec(ssm_d), bf(ssm_w_glu), vec(ssm_b_glu), vec(ssm_norm_g))
    ple = (bf(ple_w_up), bf(ple_w_gate), vec(ple_post_g))
    p_rows = p.reshape(depth, t, -1)

    h = x.reshape(t, d)
    rope = _rope(positions.reshape(t, 1).astype(F32), tm=ROPE_TILE)
    for i in range(depth):
        h, q, k, v, u = _pre(i, h, rope, *ffn1, mix_g, w_in_bf, d_attn=d_attn, tm=tm)
        attn = _attention(q.reshape(b, s, d_attn), k.reshape(b, s, d_attn), v.reshape(b, s, d_attn))
        ssm_n = _ssm(i, u.reshape(b, s, d_ssm), ssm_w_in, a_re, a_im, ssm_w_out, *ssm_rest, ts=ts)
        h = _post(i, h, attn.reshape(t, d_attn), ssm_n.reshape(t, d_ssm), p_rows, vec(attn_norm_g), w_out_bf,
                  vec(mix_post_g), *ffn2, *ple, tm=tm)
    return h.reshape(b, s, d)
```

```python
import functools

import numpy as np
import jax
import jax.numpy as jnp
from jax import lax
from jax.experimental import pallas as pl
from jax.experimental.pallas import tpu as pltpu

F32 = jnp.float32
BF16 = jnp.bfloat16

LANES = 128
SUBLANES = 8
MXU_DIM = 256

HEAD_DIM = 64
ROPE_DIM = HEAD_DIM // 4
ROPE_THETA = 500000.0
DILATED_PATTERNS = ((128, 1), (512, 4), (2048, 16))
BAND = 128
SSM_GROUP = 16
SSM_STATE = 64
SSM_CHUNK = 4
SSM_QUAD = 4
NORM_EPS = 1e-6
MASK_VALUE = -1e30

assert all(w // d == BAND for w, d in DILATED_PATTERNS)
DILATIONS = tuple(d for _, d in DILATED_PATTERNS)
ATTN_PLANES = 4
ATTN_UNROLL = 32

TOKEN_TILE = 512
ROPE_TILE = 2048
ROW_SPLITS = 2
SSM_TIME_TILE = 128
VMEM_LIMIT_BYTES = 56 * 1024 * 1024


def _rms_norm(x, g):
    ms = jnp.mean(x * x, axis=-1, keepdims=True)
    return x * lax.rsqrt(ms + NORM_EPS) * g


def _const_spec(shape):
    zeros = (0,) * len(shape)
    return pl.BlockSpec(shape, lambda *_: zeros, pipeline_mode=pl.Buffered(1))


def _layer_spec(layer, shape):
    zeros = (0,) * len(shape)
    return pl.BlockSpec((None,) + tuple(shape), lambda *_: (layer,) + zeros, pipeline_mode=pl.Buffered(1))


def _params(*sem):
    return pltpu.CompilerParams(dimension_semantics=sem, vmem_limit_bytes=VMEM_LIMIT_BYTES)


def _ffn_groups(xs, pre_ref, wg_ref, wu_ref, wd_ref, post_ref):
    xns = [_rms_norm(x, pre_ref[...]).astype(BF16) for x in xs]
    gs = [jnp.dot(xn, wg_ref[...], preferred_element_type=F32) for xn in xns]
    us = [jnp.dot(xn, wu_ref[...], preferred_element_type=F32) for xn in xns]
    mids = [(g * jax.nn.sigmoid(g) * u).astype(BF16) for g, u in zip(gs, us)]
    fs = [jnp.dot(mid, wd_ref[...], preferred_element_type=F32) for mid in mids]
    return [x + 0.5 * _rms_norm(f, post_ref[...]) for x, f in zip(xs, fs)]


def _row_groups(rows):
    sub = rows // ROW_SPLITS
    return [slice(i * sub, (i + 1) * sub) for i in range(ROW_SPLITS)]


def _rope_lane_tables():
    half = ROPE_DIM // 2
    inv_freq = ROPE_THETA ** (-jnp.arange(half, dtype=F32) * (2.0 / ROPE_DIM))
    lane = np.arange(LANES)
    in_head = lane % HEAD_DIM
    freq = jnp.where(in_head < ROPE_DIM, inv_freq[lane % half], 0.0).astype(F32)
    first = (in_head < half).astype(np.float32)
    second = ((in_head >= half) & (in_head < ROPE_DIM)).astype(np.float32)
    return freq.reshape(1, LANES), jnp.asarray(first).reshape(1, LANES), jnp.asarray(second).reshape(1, LANES)


def _rope_body(pos_ref, freq_ref, first_ref, second_ref, tab_ref):
    ang = pos_ref[...] * freq_ref[...]
    sin = jnp.sin(ang)
    tab_ref[:, :LANES] = jnp.cos(ang)
    tab_ref[:, LANES:2 * LANES] = -sin * first_ref[...]
    tab_ref[:, 2 * LANES:] = sin * second_ref[...]


def _rope(pos, *, tm):
    t = pos.shape[0]
    freq, first, second = _rope_lane_tables()
    row = lambda w: pl.BlockSpec((tm, w), lambda i: (i, 0))
    lane_tab = _const_spec((1, LANES))
    return pl.pallas_call(
        _rope_body,
        grid=(t // tm,),
        in_specs=[row(1), lane_tab, lane_tab, lane_tab],
        out_specs=row(3 * LANES),
        out_shape=jax.ShapeDtypeStruct((t, 3 * LANES), F32),
        compiler_params=_params("parallel"),
        name="rope",
    )(pos, freq, first, second)


def _pre_body(x_ref, rope_ref, pre_ref, wg_ref, wu_ref, wd_ref, post_ref,
              g_ref, w_ref, h_ref, proj_ref, *, d_attn):
    half = ROPE_DIM // 2
    scale = HEAD_DIM ** -0.5
    parts = _row_groups(x_ref.shape[0])
    hs = _ffn_groups([x_ref[rows, :] for rows in parts], pre_ref, wg_ref, wu_ref, wd_ref, post_ref)
    for rows, h in zip(parts, hs):
        h_ref[rows, :] = h
    hns = [_rms_norm(h, g_ref[...]).astype(BF16) for h in hs]
    projs = [jnp.dot(hn, w_ref[...], preferred_element_type=F32) for hn in hns]
    for rows, proj in zip(parts, projs):
        cos, sin_first, sin_second = (rope_ref[rows, j * LANES:(j + 1) * LANES] for j in range(3))

        def rotary(t):
            return (t * cos + pltpu.roll(t, LANES - half, 1) * sin_first
                    + pltpu.roll(t, half, 1) * sin_second)

        for j in range(d_attn // LANES):
            sl = slice(j * LANES, (j + 1) * LANES)
            ks = slice(d_attn + j * LANES, d_attn + (j + 1) * LANES)
            proj_ref[rows, sl] = rotary(proj[:, sl]) * scale
            proj_ref[rows, ks] = rotary(proj[:, ks])
        proj_ref[rows, 2 * d_attn:] = proj[:, 2 * d_attn:]


def _pre(layer, h, rope, pre_g, wg, wu, wd, post_g, mix_g, w_in, *, d_attn, tm):
    t, d = h.shape
    ff = wg.shape[2]
    n = w_in.shape[2]
    row = lambda w: pl.BlockSpec((tm, w), lambda i: (i, 0))
    per_layer = functools.partial(_layer_spec, layer)
    return pl.pallas_call(
        functools.partial(_pre_body, d_attn=d_attn),
        grid=(t // tm,),
        in_specs=[row(d), row(3 * LANES), per_layer((1, d)), per_layer((d, ff)),
                  per_layer((d, ff)), per_layer((ff, d)), per_layer((1, d)), per_layer((1, d)), per_layer((d, n))],
        out_specs=[row(d), row(n)],
        out_shape=[jax.ShapeDtypeStruct((t, d), F32), jax.ShapeDtypeStruct((t, n), F32)],
        compiler_params=_params("parallel"),
        name="pre",
    )(h, rope, pre_g, wg, wu, wd, post_g, mix_g, w_in)


def _attn_body(q_ref, k_ref, v_ref, o_ref, planes_s, m_s, l_s, acc_s, *, seq):
    lane = lax.broadcasted_iota(jnp.int32, (BAND, LANES), 1)
    head0 = lane < HEAD_DIM
    qi = lax.broadcasted_iota(jnp.int32, (BAND, BAND), 0)
    kj = lax.broadcasted_iota(jnp.int32, (BAND, BAND), 1)
    cur_ok = kj <= qi
    prev_ok = kj >= qi

    plane_rows = seq // ATTN_PLANES
    ones = jnp.ones((2 * BAND, LANES), BF16)

    for t, ref in enumerate((q_ref, k_ref, v_ref)):
        for r in range(ATTN_PLANES):
            planes_s[t, pl.ds(r * plane_rows, plane_rows), :] = ref[pl.ds(r, plane_rows, stride=ATTN_PLANES), :]

    for pi, d in enumerate(DILATIONS):
        nb = seq // (BAND * d)
        step = max(d // ATTN_PLANES, 1)

        def rows(start):
            return pl.ds(start, BAND) if step == 1 else pl.ds(start, BAND, stride=step)

        def load(t, start):
            return (q_ref, k_ref, v_ref)[t][rows(start), :] if d == 1 else planes_s[t, rows(start), :]

        def block(n, carry):
            r = n // nb
            i = n - r * nb
            base = r if d == 1 else (r % ATTN_PLANES) * plane_rows + r // ATTN_PLANES
            start = base + i * (BAND * step)
            prev = jnp.maximum(start - BAND * step, base)
            q = load(0, start)
            zero = jnp.zeros_like(q)
            q2 = jnp.concatenate([jnp.where(head0, q, zero), jnp.where(head0, zero, q)], axis=0)
            kk = jnp.concatenate([load(1, prev), load(1, start)], axis=0)
            vv = jnp.concatenate([load(2, prev), load(2, start)], axis=0).astype(BF16)
            s = lax.dot_general(q2.astype(BF16), kk.astype(BF16), (((1,), (1,)), ((), ())),
                                preferred_element_type=F32)
            ok = jnp.concatenate([prev_ok & (i > 0), cur_ok], axis=1)
            ok = jnp.concatenate([ok, ok], axis=0)
            s = jnp.where(ok, s, MASK_VALUE)
            m = jnp.max(s, axis=-1, keepdims=True)
            e = jnp.exp(s - m).astype(BF16)
            o2 = jnp.dot(e, jnp.concatenate([vv, ones], axis=1), preferred_element_type=F32)
            m_s[pi, rows(start), :] = jnp.where(head0, m[:BAND], m[BAND:])
            l_s[pi, rows(start), :] = jnp.where(head0, o2[:BAND, LANES:], o2[BAND:, LANES:])
            acc_s[pi, rows(start), :] = jnp.where(head0, o2[:BAND, :LANES], o2[BAND:, :LANES])
            return carry

        lax.fori_loop(0, d * nb, block, 0, unroll=ATTN_UNROLL)

    blocks_per_plane = plane_rows // BAND

    def merge(n, carry):
        r = n // blocks_per_plane
        j = n - r * blocks_per_plane
        orig = pl.ds(r + j * (BAND * ATTN_PLANES), BAND, stride=ATTN_PLANES)
        plane = pl.ds(r * plane_rows + j * BAND, BAND)
        sel = lambda ref: [ref[0, orig, :]] + [ref[pi, plane, :] for pi in range(1, len(DILATIONS))]
        ms, ls, accs = sel(m_s), sel(l_s), sel(acc_s)
        m = functools.reduce(jnp.maximum, ms)
        ws = [jnp.exp(mp - m) for mp in ms]
        num = sum(w * a for w, a in zip(ws, accs))
        den = sum(w * l for w, l in zip(ws, ls))
        o_ref[orig, :] = num / den
        return carry

    lax.fori_loop(0, seq // BAND, merge, 0, unroll=2)


def _attention(proj, d_attn):
    b, s, _ = proj.shape
    assert s % (BAND * max(DILATIONS)) == 0 and d_attn % LANES == 0
    assert all(d == 1 or d % ATTN_PLANES == 0 for d in DILATIONS)
    pairs = d_attn // LANES
    spec = pl.BlockSpec((None, s, LANES), lambda bi, hi: (bi, 0, hi))
    part = lambda t: pl.BlockSpec((None, s, LANES), lambda bi, hi: (bi, 0, t * pairs + hi))
    stats = pltpu.VMEM((len(DILATIONS), s, LANES), F32)
    return pl.pallas_call(
        functools.partial(_attn_body, seq=s),
        grid=(b, d_attn // LANES),
        in_specs=[part(0), part(1), part(2)],
        out_specs=spec,
        out_shape=jax.ShapeDtypeStruct((b, s, d_attn), F32),
        scratch_shapes=[pltpu.VMEM((3, s, LANES), F32), stats, stats, stats],
        compiler_params=_params("parallel", "parallel"),
        name="attention",
    )(proj, proj, proj)


def _ssm_matrices(lam_re, lam_im, log_dt, b_re, b_im, c_re, c_im):
    g, p = lam_re.shape
    L, Q, H = SSM_CHUNK, SSM_QUAD, SSM_GROUP
    nq = g // Q
    dt = jnp.exp(log_dt)[:, None]
    mag = jnp.exp(lam_re * dt)
    ar = mag * jnp.cos(lam_im * dt)
    ai = mag * jnp.sin(lam_im * dt)
    wr, wi = ar - 1.0, ai
    den = lam_re * lam_re + lam_im * lam_im
    cr = (wr * lam_re + wi * lam_im) / den
    ci = (wi * lam_re - wr * lam_im) / den
    bbr = cr[..., None] * b_re - ci[..., None] * b_im
    bbi = cr[..., None] * b_im + ci[..., None] * b_re
    pw_r, pw_i = [jnp.ones_like(ar)], [jnp.zeros_like(ai)]
    for _ in range(L):
        pw_r.append(pw_r[-1] * ar - pw_i[-1] * ai)
        pw_i.append(pw_r[-2] * ai + pw_i[-1] * ar)
    pr, pi = jnp.stack(pw_r), jnp.stack(pw_i)

    own = np.eye(Q, dtype=np.float32)
    mask_hp = jnp.asarray(np.kron(own, np.ones((H, p), np.float32)))
    mask_ph = jnp.asarray(np.kron(own, np.ones((p, H), np.float32)))
    mask_hh = jnp.asarray(np.kron(own, np.ones((H, H), np.float32)))

    def spread(m, mask):
        lead = m.shape[:-3]
        a, b = m.shape[-2:]
        rows = m.reshape(lead + (nq, Q * a, b))
        return jnp.tile(rows, (1,) * (len(lead) + 2) + (Q,)) * mask

    bt_r, bt_i = jnp.swapaxes(bbr, 1, 2), jnp.swapaxes(bbi, 1, 2)
    nr, ni = pr[L - 1::-1][:, :, None, :], pi[L - 1::-1][:, :, None, :]
    win_r = spread(nr * bt_r - ni * bt_i, mask_hp)
    win_i = spread(nr * bt_i + ni * bt_r, mask_hp)
    w_in = jnp.concatenate([win_r, win_i], axis=-1)
    w_in = jnp.swapaxes(w_in, 0, 1).reshape(nq, L * Q * H, 2 * Q * p).astype(BF16)

    ct_r, ct_i = jnp.swapaxes(c_re, 1, 2), jnp.swapaxes(c_im, 1, 2)
    er = ct_r * pr[1:, :, :, None] - ct_i * pi[1:, :, :, None]
    ei = ct_r * pi[1:, :, :, None] + ct_i * pr[1:, :, :, None]
    zre = spread(er, mask_ph)
    zim = spread(-ei, mask_ph)
    e0r = ct_r * pr[:L, :, :, None] - ct_i * pi[:L, :, :, None]
    e0i = ct_r * pi[:L, :, :, None] + ct_i * pr[:L, :, :, None]
    knt = jnp.einsum("gpi,ngph->ngih", bbr, e0r) - jnp.einsum("gpi,ngph->ngih", bbi, e0i)
    kq = spread(knt, mask_hh)
    zero = jnp.zeros_like(kq[0])
    xin = jnp.stack([jnp.concatenate([kq[r - k] if r >= k else zero for r in range(L)], axis=-1)
                     for k in range(L)], axis=1)
    cols = lambda z: jnp.concatenate([z[r] for r in range(L)], axis=-1)
    w_out = jnp.concatenate([cols(zre), cols(zim), xin.reshape(nq, L * Q * H, L * Q * H)], axis=1).astype(BF16)
    al_re = jnp.broadcast_to(pw_r[L].reshape(1, g * p), (SUBLANES, g * p))
    al_im = jnp.broadcast_to(pw_i[L].reshape(1, g * p), (SUBLANES, g * p))
    return w_in, w_out, al_re, al_im


def _ssm_body(u_ref, win_ref, are_ref, aim_ref, wout_ref, dskip_ref, wglu_ref, bglu_ref, g_ref,
              o_ref, z_s, v_s, io_s, *, ts):
    nb = u_ref.shape[0]
    L = SSM_CHUNK
    nc = ts // L
    nq = win_ref.shape[0]
    half = LANES // 2
    sq = win_ref.shape[2] // 2
    n_state = nq * sq
    io_slabs = io_s.shape[0]
    lane = lax.broadcasted_iota(jnp.int32, (nc * nb, LANES), 1)
    low = lane < half

    @pl.when(pl.program_id(0) == 0)
    def _():
        z_s[...] = jnp.zeros(z_s.shape, F32)

    def batch_rows(bi):
        return pl.ds(bi, ts, stride=nb)

    for bi in range(nb):
        for j in range(io_slabs):
            io_s[j, batch_rows(bi), :] = u_ref[bi, :, j * LANES:(j + 1) * LANES]
    def step_rows(j, k):
        return jnp.concatenate([io_s[j, pl.ds((m * L + k) * nb, nb), :] for m in range(nc)], axis=0)

    us = [[step_rows(j, k) for k in range(L)] for j in range(io_slabs)]
    xq = []
    for j in range(io_slabs):
        rolled = [pltpu.roll(us[j][k], half, 1) for k in range(L)]
        for hi in range(2):
            pieces = []
            for k in range(0, L, 2):
                if hi == 0:
                    pieces.append(jnp.where(low, us[j][k], rolled[k + 1]))
                else:
                    pieces.append(jnp.where(low, rolled[k], us[j][k + 1]))
            xq.append(jnp.concatenate(pieces, axis=1).astype(BF16))

    for q in range(nq):
        v = jnp.dot(xq[q], win_ref[q], preferred_element_type=F32)
        v_s[:, q * sq:(q + 1) * sq] = v[:, :sq]
        v_s[:, n_state + q * sq:n_state + (q + 1) * sq] = v[:, sq:]

    for j in range(n_state // LANES):
        re = slice(j * LANES, (j + 1) * LANES)
        im = slice(n_state + j * LANES, n_state + (j + 1) * LANES)
        ar = are_ref[:, re]
        ai = aim_ref[:, re]
        zr = z_s[:, re]
        zi = z_s[:, im]
        for m in range(nc):
            row = slice(m * nb, (m + 1) * nb)
            nzr = ar * zr - ai * zi + v_s[row, re]
            nzi = ar * zi + ai * zr + v_s[row, im]
            v_s[row, re] = zr
            v_s[row, im] = zi
            zr, zi = nzr, nzi
        z_s[:, re] = zr
        z_s[:, im] = zi

    yq = []
    for q in range(nq):
        lhs = jnp.concatenate([v_s[:, q * sq:(q + 1) * sq].astype(BF16),
                               v_s[:, n_state + q * sq:n_state + (q + 1) * sq].astype(BF16), xq[q]], axis=1)
        yq.append(jnp.dot(lhs, wout_ref[q], preferred_element_type=F32))
    y_steps, u_steps = [], []
    for r in range(L):
        slabs = []
        for j in range(io_slabs):
            lo_src = yq[2 * j][:, (r // 2) * LANES:(r // 2 + 1) * LANES]
            hi_src = yq[2 * j + 1][:, (r // 2) * LANES:(r // 2 + 1) * LANES]
            if r % 2 == 0:
                slabs.append(jnp.where(low, lo_src, pltpu.roll(hi_src, half, 1)))
            else:
                slabs.append(jnp.where(low, pltpu.roll(lo_src, half, 1), hi_src))
        y_steps.append(jnp.concatenate(slabs, axis=1))
        u_steps.append(jnp.concatenate([us[j][r] for j in range(io_slabs)], axis=1))
    y = jnp.concatenate(y_steps, axis=0)
    u = jnp.concatenate(u_steps, axis=0)
    y = jax.nn.gelu(y + dskip_ref[...] * u)
    gate = jnp.dot(y.astype(BF16), wglu_ref[...], preferred_element_type=F32) + bglu_ref[...]
    y = _rms_norm(y * jax.nn.sigmoid(gate), g_ref[...])
    for r in range(L):
        for m in range(nc):
            src = (r * nc + m) * nb
            for j in range(io_slabs):
                io_s[j, pl.ds((m * L + r) * nb, nb), :] = y[src:src + nb, j * LANES:(j + 1) * LANES]
    for bi in range(nb):
        o_ref[bi] = jnp.concatenate([io_s[j, batch_rows(bi), :] for j in range(io_slabs)],
                                    axis=1).astype(o_ref.dtype)


def _ssm(layer, u, w_in, a_re, a_im, w_out, d_skip, w_glu, b_glu, norm_g, *, ts):
    b, s, n_proj = u.shape
    d_ssm = w_glu.shape[2]
    assert n_proj % d_ssm == 0
    assert b == SUBLANES, "the S5 scan keeps the batch on the sublane axis"
    assert SSM_CHUNK * SSM_QUAD * SSM_GROUP == MXU_DIM and SSM_QUAD * SSM_GROUP * 2 == LANES and SSM_CHUNK % 2 == 0
    n_state = a_re.shape[2]
    tile = pl.BlockSpec((b, ts, d_ssm), lambda i: (0, i, 0))
    u_tile = pl.BlockSpec((b, ts, d_ssm), lambda i: (0, i, n_proj // d_ssm - 1))
    per_layer = functools.partial(_layer_spec, layer)
    return pl.pallas_call(
        functools.partial(_ssm_body, ts=ts),
        grid=(s // ts,),
        in_specs=[u_tile, per_layer(w_in.shape[1:]), per_layer(a_re.shape[1:]), per_layer(a_im.shape[1:]),
                  per_layer(w_out.shape[1:]), per_layer((1, d_ssm)), per_layer((d_ssm, d_ssm)),
                  per_layer((1, d_ssm)), per_layer((1, d_ssm))],
        out_specs=tile,
        out_shape=jax.ShapeDtypeStruct((b, s, d_ssm), BF16),
        scratch_shapes=[pltpu.VMEM((b, 2 * n_state), F32),
                        pltpu.VMEM((ts // SSM_CHUNK * b, 2 * n_state), F32),
                        pltpu.VMEM((d_ssm // LANES, ts * b, LANES), F32)],
        compiler_params=_params("arbitrary"),
        name="ssm",
    )(u, w_in, a_re, a_im, w_out, d_skip, w_glu, b_glu, norm_g)


def _post_body(h_ref, attn_ref, ssm_ref, p_ref, ga_ref, wa_ref, ws_ref, mixpost_ref,
               pre_ref, wg_ref, wu_ref, wd_ref, post_ref, wup_ref, wgate_ref, plepost_ref, o_ref):
    parts = _row_groups(h_ref.shape[0])
    ans = [_rms_norm(attn_ref[rows, :], ga_ref[...]).astype(BF16) for rows in parts]
    mixes = [jnp.dot(an, wa_ref[...], preferred_element_type=F32)
             + jnp.dot(ssm_ref[rows, :], ws_ref[...], preferred_element_type=F32) for an, rows in zip(ans, parts)]
    xs = [h_ref[rows, :] + _rms_norm(mix, mixpost_ref[...]) for rows, mix in zip(parts, mixes)]
    hs = _ffn_groups(xs, pre_ref, wg_ref, wu_ref, wd_ref, post_ref)
    ups = [jnp.dot(p_ref[rows, :].astype(BF16), wup_ref[...], preferred_element_type=F32) for rows in parts]
    gates = [jnp.dot(h.astype(BF16), wgate_ref[...], preferred_element_type=F32) for h in hs]
    for rows, h, up, gate in zip(parts, hs, ups, gates):
        o_ref[rows, :] = h + _rms_norm(up * jax.nn.sigmoid(gate), plepost_ref[...])


def _post(layer, h, attn, ssm_n, p, attn_g, w_out, mix_post_g, pre_g, wg, wu, wd, post_g, w_up, w_gate,
          ple_post_g, *, tm):
    t, d = h.shape
    da, ds, dp = attn.shape[1], ssm_n.shape[1], p.shape[2]
    ff = wg.shape[2]
    row = lambda w: pl.BlockSpec((tm, w), lambda i: (i, 0))
    per_layer = functools.partial(_layer_spec, layer)
    vec = per_layer((1, d))
    w_attn = pl.BlockSpec((None, da, d), lambda i: (layer, 0, 0), pipeline_mode=pl.Buffered(1))
    w_ssm = pl.BlockSpec((None, ds, d), lambda i: (layer, da // ds, 0), pipeline_mode=pl.Buffered(1))
    return pl.pallas_call(
        _post_body,
        grid=(t // tm,),
        in_specs=[row(d), row(da), row(ds), pl.BlockSpec((None, tm, dp), lambda i: (layer, i, 0)),
                  per_layer((1, da)), w_attn, w_ssm, vec, vec, per_layer((d, ff)), per_layer((d, ff)),
                  per_layer((ff, d)), vec, per_layer((dp, d)), per_layer((d, d)), vec],
        out_specs=row(d),
        out_shape=jax.ShapeDtypeStruct((t, d), F32),
        compiler_params=_params("parallel"),
        name="post",
    )(h, attn, ssm_n, p, attn_g, w_out, w_out, mix_post_g, pre_g, wg, wu, wd, post_g, w_up, w_gate, ple_post_g)


def kernel(x, p, positions, ffn1_pre_g, ffn1_w_gate, ffn1_w_up, ffn1_w_down, ffn1_post_g, mix_pre_g, w_in, attn_norm_g, ssm_lam_re, ssm_lam_im, ssm_log_dt, ssm_b_re, ssm_b_im, ssm_c_re, ssm_c_im, ssm_d, ssm_w_glu, ssm_b_glu, ssm_norm_g, w_out, mix_post_g, ffn2_pre_g, ffn2_w_gate, ffn2_w_up, ffn2_w_down, ffn2_post_g, ple_w_up, ple_w_gate, ple_post_g):
    b, s, d = x.shape
    depth = p.shape[0]
    t = b * s
    d_attn = attn_norm_g.shape[1]
    d_ssm = ssm_norm_g.shape[1]
    assert d_attn % d_ssm == 0
    tm = min(TOKEN_TILE, t)
    ts = min(SSM_TIME_TILE, s)
    vec = lambda a: a.reshape(depth, 1, -1)
    bf = lambda a: a.astype(BF16)

    ffn1 = (vec(ffn1_pre_g), bf(ffn1_w_gate), bf(ffn1_w_up), bf(ffn1_w_down), vec(ffn1_post_g))
    ffn2 = (vec(ffn2_pre_g), bf(ffn2_w_gate), bf(ffn2_w_up), bf(ffn2_w_down), vec(ffn2_post_g))
    mix_g, w_in_bf, w_out_bf = vec(mix_pre_g), bf(w_in), bf(w_out)
    ssm_w_in, ssm_w_out, a_re, a_im = jax.vmap(_ssm_matrices)(
        ssm_lam_re, ssm_lam_im, ssm_log_dt, ssm_b_re, ssm_b_im, ssm_c_re, ssm_c_im)
    ssm_rest = (vec(ssm_d), bf(ssm_w_glu), vec(ssm_b_glu), vec(ssm_norm_g))
    ple = (bf(ple_w_up), bf(ple_w_gate), vec(ple_post_g))
    p_rows = p.reshape(depth, t, -1)

    h = x.reshape(t, d)
    rope = _rope(positions.reshape(t, 1).astype(F32), tm=ROPE_TILE)
    for i in range(depth):
        h, proj = _pre(i, h, rope, *ffn1, mix_g, w_in_bf, d_attn=d_attn, tm=tm)
        proj = proj.reshape(b, s, -1)
        attn = _attention(proj, d_attn)
        ssm_n = _ssm(i, proj, ssm_w_in, a_re, a_im, ssm_w_out, *ssm_rest, ts=ts)
        h = _post(i, h, attn.reshape(t, d_attn), ssm_n.reshape(t, d_ssm), p_rows, vec(attn_norm_g), w_out_bf,
                  vec(mix_post_g), *ffn2, *ple, tm=tm)
    return h.reshape(b, s, d)
```
